```python
import math, functools
import jax, jax.numpy as jnp
from jax import lax
import numpy as np

D_MODEL = 1024
BATCH = 4
SEQ = 8192
DEPTH = 2
DEC_BATCH = 32
DEC_SEQ = 4
PAST_LEN = 16384
PAGE_SIZE = 128

D_POOL = D_MODEL // 4
POOL_WINDOWS = (2, 4, 8, 16)
POOL_GROUPS = len(POOL_WINDOWS)
POOL_GDIM = D_POOL // POOL_GROUPS
POOL_STATE = max(POOL_WINDOWS) - 1
N_HEADS_B = 8
HEAD_DIM = 64
D_ATTN = N_HEADS_B * HEAD_DIM
MOBA_BLOCK = 256
MOBA_TOPK = 3
Q_CHUNK = 128
D_CONV = D_MODEL // 4
CONV_W = 31
CONV_STATE = CONV_W - 1
N_BRANCH = 3
REL_BUCKETS = 32
REL_MAX_DIST = 128
DN_ALPHA = (2 * DEPTH) ** 0.25
DN_BETA = (8 * DEPTH) ** -0.25
LN_EPS = 1e-5
NEG = -1e30
IN_SPLITS = (D_POOL, D_POOL, D_ATTN, D_ATTN, D_ATTN, D_ATTN, D_CONV, D_CONV, D_CONV, D_MODEL, D_MODEL, D_MODEL)
D_IN = sum(IN_SPLITS)
SPLIT_IDX = tuple(int(i) for i in np.cumsum(IN_SPLITS)[:-1])

kernel_name = 'gated_parallel_pool_moba_conv_decoder_step'


def _layer_norm(x, g, b):
    xf = x.astype(jnp.float32)
    mu = jnp.mean(xf, -1, keepdims=True)
    var = jnp.mean(jnp.square(xf - mu), -1, keepdims=True)
    return ((xf - mu) * lax.rsqrt(var + LN_EPS) * g + b).astype(x.dtype)


def _t5_bucket(rel):
    n = jnp.maximum(rel, 0)
    max_exact = REL_BUCKETS // 2
    nf = jnp.maximum(n, 1).astype(jnp.float32)
    large = max_exact + (jnp.log(nf / max_exact) / math.log(REL_MAX_DIST / max_exact)
                         * (REL_BUCKETS - max_exact)).astype(jnp.int32)
    large = jnp.minimum(large, REL_BUCKETS - 1)
    return jnp.where(n < max_exact, n, large)


def _pool_mixer(u, u_prev, pos, w_pool, s_pool):
    b, t, _ = u.shape
    p = u_prev.shape[1]
    ext = jnp.concatenate([u_prev.astype(u.dtype), u], axis=1)
    cs = jnp.pad(jnp.cumsum(ext.astype(jnp.float32), axis=1), ((0, 0), (1, 0), (0, 0)))
    means = []
    for g, w in enumerate(POOL_WINDOWS):
        lo, hi = g * POOL_GDIM, (g + 1) * POOL_GDIM
        s = cs[:, p + 1:p + 1 + t, lo:hi] - cs[:, p + 1 - w:p + 1 - w + t, lo:hi]
        cnt = jnp.minimum(pos + 1, w).astype(jnp.float32)[None, :, None]
        means.append(s / cnt)
    pooled = jnp.concatenate(means, -1).astype(u.dtype)
    r = (pooled - u).reshape(b, t, POOL_GROUPS, POOL_GDIM)
    y = jnp.einsum('btgc,gcd->btgd', r, w_pool).reshape(b, t, D_POOL) * s_pool
    return y, ext[:, ext.shape[1] - p:]


def _conv_mixer(a, a_prev, w_dw, b_dw, g, bb, w_pw2):
    ext = jnp.concatenate([a_prev.astype(a.dtype), a], axis=1)
    h = lax.conv_general_dilated(ext, w_dw[:, None, :].astype(ext.dtype), window_strides=(1,),
                                 padding='VALID', dimension_numbers=('NWC', 'WIO', 'NWC'),
                                 feature_group_count=D_CONV) + b_dw
    h = jax.nn.silu(_layer_norm(h, g, bb))
    return h @ w_pw2, ext[:, ext.shape[1] - CONV_STATE:]


def _to_blocks(k):
    bk, l, h, d = k.shape
    nb = -(-l // MOBA_BLOCK)
    k = jnp.pad(k, ((0, 0), (0, nb * MOBA_BLOCK - l), (0, 0), (0, 0)))
    return k.reshape(bk, nb, MOBA_BLOCK, h, d).transpose(0, 3, 1, 2, 4)


def _moba_attend(q, q_pos, kb, vb, kmean, rel_bias):
    bq, hq, tq, _ = q.shape
    nb = kb.shape[2]
    n_sel = min(MOBA_TOPK, nb)
    own = q_pos // MOBA_BLOCK
    gate = jnp.einsum('bhqd,bhnd->bhqn', q.astype(jnp.float32), kmean)
    past = jnp.arange(nb)[None, None, None, :] < own[None, None, :, None]
    gate = jnp.where(past, gate, NEG)
    _, sel = lax.top_k(gate, n_sel)
    sel_ok = sel < own[None, None, :, None]
    own_b = jnp.broadcast_to(own[None, None, :, None], (bq, hq, tq, 1)).astype(sel.dtype)
    blocks = jnp.concatenate([sel, own_b], -1)
    slot_ok = jnp.concatenate([sel_ok, jnp.ones((bq, hq, tq, 1), bool)], -1)
    bi = jnp.arange(bq)[:, None, None, None]
    hi = jnp.arange(hq)[None, :, None, None]
    kg = kb[bi, hi, blocks]
    vg = vb[bi, hi, blocks]
    k_pos = blocks[..., None] * MOBA_BLOCK + jnp.arange(MOBA_BLOCK)
    rel = q_pos[None, None, :, None, None] - k_pos
    mask = slot_ok[..., None] & (rel >= 0)
    bias = rel_bias.T.astype(jnp.float32)[hi[..., None], _t5_bucket(rel)]
    logits = jnp.einsum('bhqd,bhqskd->bhqsk', q, kg).astype(jnp.float32) * (HEAD_DIM ** -0.5) + bias
    logits = jnp.where(mask, logits, NEG)
    p = jax.nn.softmax(logits.reshape(bq, hq, tq, -1), axis=-1).reshape(logits.shape)
    return jnp.einsum('bhqsk,bhqskd->bhqd', p.astype(vg.dtype), vg)


def _prompt_moba(q, k, v, rel_bias):
    b, t, h, d = q.shape
    kb, vb = _to_blocks(k), _to_blocks(v)
    kmean = jnp.mean(kb, axis=3, dtype=jnp.float32)
    nc = t // Q_CHUNK
    qc = q.transpose(0, 2, 1, 3).reshape(b, h, nc, Q_CHUNK, d).transpose(2, 0, 1, 3, 4)
    pc = jnp.arange(t, dtype=jnp.int32).reshape(nc, Q_CHUNK)
    out = lax.map(lambda a: _moba_attend(a[0], a[1], kb, vb, kmean, rel_bias), (qc, pc))
    return out.transpose(1, 0, 3, 2, 4).reshape(b, t, h, d)


def _sample_moba(q, k, v, k_past, v_past, pos0, rel_bias):
    t = q.shape[1]
    kb = _to_blocks(jnp.concatenate([k_past.astype(k.dtype), k], axis=1))
    vb = _to_blocks(jnp.concatenate([v_past.astype(v.dtype), v], axis=1))
    kmean = jnp.mean(kb, axis=3, dtype=jnp.float32)
    q_pos = pos0 + jnp.arange(t, dtype=jnp.int32)
    o = _moba_attend(q.transpose(0, 2, 1, 3), q_pos, kb, vb, kmean, rel_bias)
    return o.transpose(0, 2, 1, 3)


def _trunk_layer(x, pos0, pool_prev, conv_prev, attend, w_in, w_pool, s_pool, w_dw, b_dw,
                 cln_g, cln_b, w_pw2, p_a, p_b, p_c, w_out, ln_g, ln_b):
    bx, t, _ = x.shape
    z = x @ w_in
    (u_a, g_a, q, k, v, g_b, c_val, c_gate, g_c, m_a, m_b, m_c) = jnp.split(z, SPLIT_IDX, axis=-1)
    pos = pos0 + jnp.arange(t, dtype=jnp.int32)
    ya, pool_state = _pool_mixer(u_a, pool_prev, pos, w_pool, s_pool)
    ya = ya * jax.nn.silu(g_a)
    kh = k.reshape(bx, t, N_HEADS_B, HEAD_DIM)
    vh = v.reshape(bx, t, N_HEADS_B, HEAD_DIM)
    yb = attend(q.reshape(bx, t, N_HEADS_B, HEAD_DIM), kh, vh).reshape(bx, t, D_ATTN)
    yb = yb * jax.nn.silu(g_b)
    a = c_val * jax.nn.sigmoid(c_gate)
    yc, conv_state = _conv_mixer(a, conv_prev, w_dw, b_dw, cln_g, cln_b, w_pw2)
    yc = yc * jax.nn.silu(g_c)
    h = (jax.nn.sigmoid(m_a) * (ya @ p_a) + jax.nn.sigmoid(m_b) * (yb @ p_b)
         + jax.nn.sigmoid(m_c) * (yc @ p_c))
    x_new = _layer_norm(DN_ALPHA * x + h @ w_out, ln_g, ln_b)
    return x_new, kh, vh, pool_state, conv_state


def setup_inputs(seed: int = 0) -> dict:
    key = jax.random.key(seed)
    ks = jax.random.split(key, 24)
    f32 = jnp.float32

    def nrm(k, shape, s):
        return jax.random.normal(k, shape, f32) * s

    n_pages = PAST_LEN // PAGE_SIZE
    n_used = DEC_BATCH * n_pages
    n_pool = n_used + n_used // 4
    x_prompt = nrm(ks[0], (BATCH, SEQ, D_MODEL), 1.0)
    x_sample = nrm(ks[1], (DEC_BATCH, DEC_SEQ, D_MODEL), 1.0)
    cache_k = nrm(ks[2], (n_pool, DEPTH, PAGE_SIZE, N_HEADS_B, HEAD_DIM), 1.0)
    cache_v = nrm(ks[3], (n_pool, DEPTH, PAGE_SIZE, N_HEADS_B, HEAD_DIM), 1.0)
    page_table = jax.random.permutation(ks[4], n_pool)[:n_used].reshape(DEC_BATCH, n_pages).astype(jnp.int32)
    state_pool = nrm(ks[5], (DEC_BATCH, DEPTH, POOL_STATE, D_POOL), 1.0)
    state_conv = nrm(ks[6], (DEC_BATCH, DEPTH, CONV_STATE, D_CONV), 0.5)
    rel_bias = nrm(ks[7], (REL_BUCKETS, N_HEADS_B), 0.5)
    w_in = nrm(ks[8], (DEPTH, D_MODEL, D_IN), D_MODEL ** -0.5)
    w_pool = nrm(ks[9], (DEPTH, POOL_GROUPS, POOL_GDIM, POOL_GDIM), POOL_GDIM ** -0.5)
    s_pool = 1.0 + nrm(ks[10], (DEPTH, D_POOL), 0.1)
    w_dw = nrm(ks[11], (DEPTH, CONV_W, D_CONV), CONV_W ** -0.5)
    b_dw = nrm(ks[12], (DEPTH, D_CONV), 0.02)
    conv_ln_g = 1.0 + nrm(ks[13], (DEPTH, D_CONV), 0.02)
    conv_ln_b = nrm(ks[14], (DEPTH, D_CONV), 0.02)
    w_pw2 = nrm(ks[15], (DEPTH, D_CONV, D_CONV), D_CONV ** -0.5)
    p_a = nrm(ks[16], (DEPTH, D_POOL, D_MODEL), D_POOL ** -0.5 * DN_BETA)
    p_b = nrm(ks[17], (DEPTH, D_ATTN, D_MODEL), D_ATTN ** -0.5 * DN_BETA)
    p_c = nrm(ks[18], (DEPTH, D_CONV, D_MODEL), D_CONV ** -0.5 * DN_BETA)
    w_out = nrm(ks[19], (DEPTH, D_MODEL, D_MODEL), D_MODEL ** -0.5 * DN_BETA)
    ln_g = 1.0 + nrm(ks[20], (DEPTH, D_MODEL), 0.02)
    ln_b = nrm(ks[21], (DEPTH, D_MODEL), 0.02)
    return {'x_prompt': x_prompt, 'x_sample': x_sample, 'cache_k': cache_k, 'cache_v': cache_v,
            'page_table': page_table, 'state_pool': state_pool, 'state_conv': state_conv,
            'rel_bias': rel_bias, 'w_in': w_in, 'w_pool': w_pool, 's_pool': s_pool, 'w_dw': w_dw,
            'b_dw': b_dw, 'conv_ln_g': conv_ln_g, 'conv_ln_b': conv_ln_b, 'w_pw2': w_pw2,
            'p_a': p_a, 'p_b': p_b, 'p_c': p_c, 'w_out': w_out, 'ln_g': ln_g, 'ln_b': ln_b}


def reference(x_prompt, x_sample, cache_k, cache_v, page_table, state_pool, state_conv, rel_bias,
              w_in, w_pool, s_pool, w_dw, b_dw, conv_ln_g, conv_ln_b, w_pw2, p_a, p_b, p_c,
              w_out, ln_g, ln_b):
    dec_b = x_sample.shape[0]
    past_len = page_table.shape[1] * cache_k.shape[2]
    xp, xs = x_prompt, x_sample
    pool0 = jnp.zeros((xp.shape[0], POOL_STATE, D_POOL), xp.dtype)
    conv0 = jnp.zeros((xp.shape[0], CONV_STATE, D_CONV), xp.dtype)
    kp_l, vp_l, ks_l, vs_l, pp_l, ps_l, cp_l, cs_l = [], [], [], [], [], [], [], []
    for l in range(DEPTH):
        wl = (w_in[l], w_pool[l], s_pool[l], w_dw[l], b_dw[l], conv_ln_g[l], conv_ln_b[l], w_pw2[l],
              p_a[l], p_b[l], p_c[l], w_out[l], ln_g[l], ln_b[l])
        xp, kp, vp, pp, cp = _trunk_layer(xp, 0, pool0, conv0,
                                          functools.partial(_prompt_moba, rel_bias=rel_bias), *wl)
        k_past = cache_k[page_table, l].reshape(dec_b, past_len, N_HEADS_B, HEAD_DIM)
        v_past = cache_v[page_table, l].reshape(dec_b, past_len, N_HEADS_B, HEAD_DIM)
        attend_s = functools.partial(_sample_moba, k_past=k_past, v_past=v_past, pos0=past_len,
                                     rel_bias=rel_bias)
        xs, ks_, vs_, ps_, cs_ = _trunk_layer(xs, past_len, state_pool[:, l], state_conv[:, l],
                                              attend_s, *wl)
        kp_l.append(kp); vp_l.append(vp); ks_l.append(ks_); vs_l.append(vs_)
        pp_l.append(pp); ps_l.append(ps_); cp_l.append(cp); cs_l.append(cs_)
    new_k_prompt = jnp.stack(kp_l, axis=1)
    new_v_prompt = jnp.stack(vp_l, axis=1)
    new_k_sample = jnp.stack(ks_l, axis=1)
    new_v_sample = jnp.stack(vs_l, axis=1)
    new_pool_prompt = jnp.stack(pp_l, axis=1)
    new_pool_sample = jnp.stack(ps_l, axis=1)
    new_conv_prompt = jnp.stack(cp_l, axis=1)
    new_conv_sample = jnp.stack(cs_l, axis=1)
    return (xp, xs, new_k_prompt, new_v_prompt, new_k_sample, new_v_sample,
            new_pool_prompt, new_pool_sample, new_conv_prompt, new_conv_sample)
```

```python
import functools
import math

import numpy as np
import jax
import jax.numpy as jnp
from jax import lax
from jax.experimental import pallas as pl
from jax.experimental.pallas import tpu as pltpu

POOL_WINDOWS = (2, 4, 8, 16)
POOL_HALO = 16
CONV_W = 31
CONV_HALO = 32
N_HEADS = 8
HEAD_DIM = 64
MOBA_BLOCK = 256
MOBA_TOPK = 3
REL_MAX_DIST = 128
LN_EPS = 1e-5
NEG = -1e30
LANES = 128
VMEM_LIMIT = 56 * 1024 * 1024

F32 = jnp.float32
BF16 = jnp.bfloat16


def _sigmoid(x):
    return 1.0 / (1.0 + jnp.exp(-x))


def _silu(x):
    return x * _sigmoid(x)


def _layer_norm(x, g, b):
    mu = jnp.mean(x, axis=-1, keepdims=True)
    xc = x - mu
    var = jnp.mean(xc * xc, axis=-1, keepdims=True)
    return xc * lax.rsqrt(var + LN_EPS) * g + b


def _bucket_table(n_buckets, n):
    rel = np.arange(n)
    max_exact = n_buckets // 2
    nf = np.maximum(rel, 1).astype(np.float32)
    large = max_exact + (np.log(nf / np.float32(max_exact)) / np.float32(math.log(REL_MAX_DIST / max_exact))
                         * np.float32(n_buckets - max_exact)).astype(np.int32)
    large = np.minimum(large, n_buckets - 1)
    return np.where(rel < max_exact, rel, large).astype(np.int32)


def _splits(d_pool, d_attn, d_conv, d_model):
    names = ("u_a", "g_a", "q", "k", "v", "g_b", "c_val", "c_gate", "g_c", "m_a", "m_b", "m_c")
    widths = (d_pool, d_pool, d_attn, d_attn, d_attn, d_attn, d_conv, d_conv, d_conv, d_model, d_model, d_model)
    off, out = 0, {}
    for nme, w in zip(names, widths):
        out[nme] = (off, off + w)
        off += w
    return out


def _branches(seg, extu_ref, exta_ref, cnt, wpool_ref, spool_ref, wdw_ref, bdw_ref, clng_ref, clnb_ref,
              wpw2_ref, pa_ref, pc_ref, *, tm, rs, d_pool):
    u = seg("u_a")
    extu_ref[pl.ds(POOL_HALO * rs, tm), :] = u
    gdim = d_pool // len(POOL_WINDOWS)
    lane = lax.broadcasted_iota(jnp.int32, (1, d_pool), 1)
    acc = u
    num = None
    j = 1
    for g, w in enumerate(POOL_WINDOWS):
        while j < w:
            acc = acc + extu_ref[pl.ds((POOL_HALO - j) * rs, tm), :]
            j += 1
        num = acc if num is None else jnp.where(lane >= g * gdim, acc, num)
    pooled = num / cnt
    r = pooled - u
    ya = jnp.dot(r.astype(BF16), wpool_ref[...], preferred_element_type=F32) * spool_ref[...]
    ya = ya * _silu(seg("g_a"))
    hac = _sigmoid(seg("m_a")) * jnp.dot(ya.astype(BF16), pa_ref[...], preferred_element_type=F32)

    a = seg("c_val") * _sigmoid(seg("c_gate"))
    exta_ref[pl.ds(CONV_HALO * rs, tm), :] = a
    h = jnp.zeros_like(a) + bdw_ref[...]
    for j in range(CONV_W):
        h = h + wdw_ref[j:j + 1, :] * exta_ref[pl.ds((CONV_HALO - (CONV_W - 1) + j) * rs, tm), :]
    h = _silu(_layer_norm(h, clng_ref[...], clnb_ref[...]))
    yc = jnp.dot(h.astype(BF16), wpw2_ref[...], preferred_element_type=F32) * _silu(seg("g_c"))
    hac = hac + _sigmoid(seg("m_c")) * jnp.dot(yc.astype(BF16), pc_ref[...], preferred_element_type=F32)
    return hac


def _merge(yb, gbs, hac, smb, x, pb_ref, wout_ref, lng_ref, lnb_ref, alpha):
    ybg = (yb * gbs).astype(BF16)
    h = hac + smb * jnp.dot(ybg, pb_ref[...], preferred_element_type=F32)
    y = alpha * x + jnp.dot(h.astype(BF16), wout_ref[...], preferred_element_type=F32)
    return _layer_norm(y, lng_ref[...], lnb_ref[...])


def _proj_prompt_kernel(x_ref, win_ref, wpool_ref, spool_ref, wdw_ref, bdw_ref, clng_ref, clnb_ref, wpw2_ref,
                        pa_ref, pc_ref,
                        k_ref, v_ref, kb_ref, vt_ref, qt_ref, kmean_ref, gbs_ref, smb_ref, hac_ref, pst_ref, cst_ref,
                        extu_ref, exta_ref, *, tm, sp, d_pool, n_tiles):
    i = pl.program_id(1)
    xb = x_ref[0].astype(BF16)

    def seg(name):
        a, b = sp[name]
        return jnp.dot(xb, win_ref[:, a:b], preferred_element_type=F32)

    @pl.when(i == 0)
    def _():
        extu_ref[0:POOL_HALO, :] = jnp.zeros((POOL_HALO, d_pool), F32)
        exta_ref[0:CONV_HALO, :] = jnp.zeros((CONV_HALO, exta_ref.shape[1]), F32)

    pos1 = i * tm + lax.broadcasted_iota(jnp.int32, (tm, 1), 0) + 1
    wl = jnp.zeros((1, d_pool), jnp.int32)
    lane = lax.broadcasted_iota(jnp.int32, (1, d_pool), 1)
    gdim = d_pool // len(POOL_WINDOWS)
    for g, w in enumerate(POOL_WINDOWS):
        wl = jnp.where(lane >= g * gdim, w, wl)
    cnt = jnp.minimum(pos1, wl).astype(F32)

    hac_ref[0] = _branches(seg, extu_ref, exta_ref, cnt, wpool_ref, spool_ref, wdw_ref, bdw_ref, clng_ref,
                           clnb_ref, wpw2_ref, pa_ref, pc_ref, tm=tm, rs=1, d_pool=d_pool)

    @pl.when(i == n_tiles - 1)
    def _():
        pst_ref[0] = extu_ref[pl.ds(tm + 1, POOL_HALO - 1), :]
        cst_ref[0] = exta_ref[pl.ds(tm + CONV_HALO - (CONV_W - 1), CONV_W - 1), :]

    extu_ref[0:POOL_HALO, :] = extu_ref[pl.ds(tm, POOL_HALO), :]
    exta_ref[0:CONV_HALO, :] = exta_ref[pl.ds(tm, CONV_HALO), :]

    q = seg("q")
    qt_ref[0, 0] = q.T
    k = seg("k")
    k_ref[0] = k
    kb_ref[0, 0] = k.astype(BF16)
    kmean_ref[0, 0] = jnp.sum(k, axis=0, keepdims=True) * (1.0 / tm)
    v = seg("v")
    v_ref[0] = v
    vt_ref[0, 0] = v.T.astype(BF16)
    gbs_ref[0] = _silu(seg("g_b"))
    smb_ref[0] = _sigmoid(seg("m_b"))


def _const_spec(shape):
    nd = len(shape)
    return pl.BlockSpec(shape, lambda *_: (0,) * nd, pipeline_mode=pl.Buffered(1))


def _proj_prompt(x, win, wpool, spool, wdw, bdw, clng, clnb, wpw2, pa, pc, sp):
    bsz, t, d_model = x.shape
    tm = MOBA_BLOCK
    n_tiles = t // tm
    d_pool, d_conv, d_attn = wpool.shape[0], wpw2.shape[0], sp["q"][1] - sp["q"][0]
    kern = functools.partial(_proj_prompt_kernel, tm=tm, sp=sp, d_pool=d_pool, n_tiles=n_tiles)
    row = lambda w: pl.BlockSpec((1, tm, w), lambda b, i: (b, i, 0))
    blk = lambda r, c: pl.BlockSpec((1, 1, r, c), lambda b, i: (b, i, 0, 0))
    out_shape = (
        jax.ShapeDtypeStruct((bsz, t, d_attn), F32),
        jax.ShapeDtypeStruct((bsz, t, d_attn), F32),
        jax.ShapeDtypeStruct((bsz, n_tiles, tm, d_attn), BF16),
        jax.ShapeDtypeStruct((bsz, n_tiles, d_attn, tm), BF16),
        jax.ShapeDtypeStruct((bsz, n_tiles, d_attn, tm), F32),
        jax.ShapeDtypeStruct((bsz, n_tiles, 1, d_attn), F32),
        jax.ShapeDtypeStruct((bsz, t, d_attn), F32),
        jax.ShapeDtypeStruct((bsz, t, d_model), F32),
        jax.ShapeDtypeStruct((bsz, t, d_model), F32),
        jax.ShapeDtypeStruct((bsz, POOL_HALO - 1, d_pool), F32),
        jax.ShapeDtypeStruct((bsz, CONV_W - 1, d_conv), F32),
    )
    out_specs = (
        row(d_attn), row(d_attn), blk(tm, d_attn), blk(d_attn, tm), blk(d_attn, tm), blk(1, d_attn),
        row(d_attn), row(d_model), row(d_model),
        pl.BlockSpec((1, POOL_HALO - 1, d_pool), lambda b, i: (b, 0, 0)),
        pl.BlockSpec((1, CONV_W - 1, d_conv), lambda b, i: (b, 0, 0)),
    )
    in_specs = [row(d_model)] + [_const_spec(a.shape) for a in (win, wpool, spool, wdw, bdw, clng, clnb, wpw2, pa, pc)]
    return pl.pallas_call(
        kern, grid=(bsz, n_tiles), in_specs=in_specs, out_specs=out_specs, out_shape=out_shape,
        scratch_shapes=[pltpu.VMEM((POOL_HALO + tm, d_pool), F32), pltpu.VMEM((CONV_HALO + tm, d_conv), F32)],
        compiler_params=pltpu.CompilerParams(dimension_semantics=("arbitrary", "arbitrary"),
                                             vmem_limit_bytes=VMEM_LIMIT),
        name="proj_prompt",
    )(x, win, wpool, spool, wdw, bdw, clng, clnb, wpw2, pa, pc)


def _attn_prompt_kernel(cfar_ref, qt_ref, kb_ref, vt_ref, kmean_ref, bown_ref, bprev_ref, x_ref, hac_ref, smb_ref,
                        gbs_ref, pb_ref, wout_ref, lng_ref, lnb_ref, o_ref,
                        msk_ref, m_ref, l_ref, acc_ref, yb_ref, *, nb, tq, alpha):
    i = pl.program_id(1)
    n_iota = lax.broadcasted_iota(jnp.int32, (nb, tq), 0)
    past = n_iota < i
    half = lax.broadcasted_iota(jnp.int32, (2 * HEAD_DIM, tq), 0) // HEAD_DIM
    scale = HEAD_DIM ** -0.5

    for h in range(N_HEADS):
        hp, e = divmod(h, 2)
        lanes = slice(hp * 2 * HEAD_DIM, (hp + 1) * 2 * HEAD_DIM)
        rows = slice(h * HEAD_DIM, (h + 1) * HEAD_DIM)
        cfar = cfar_ref[h]
        qh = jnp.where(half == e, qt_ref[0, 0, lanes, :], 0.0)

        gate = jnp.dot(kmean_ref[0, :, lanes], qh, preferred_element_type=F32,
                       precision=lax.Precision.HIGHEST)
        gw = jnp.where(past, gate, NEG)
        sel = jnp.zeros((nb, tq), jnp.bool_)
        for _ in range(min(MOBA_TOPK, nb)):
            mx = jnp.max(gw, axis=0, keepdims=True)
            idx = jnp.min(jnp.where(gw == mx, n_iota, nb), axis=0, keepdims=True)
            pick = n_iota == idx
            sel = jnp.logical_or(sel, pick)
            gw = jnp.where(pick, -jnp.inf, gw)
        msk_ref[...] = jnp.where(jnp.logical_and(sel, past), 0.0, NEG)

        qb = (qh * scale).astype(BF16)

        def scores(n):
            return jnp.dot(kb_ref[0, n, :, lanes], qb, preferred_element_type=F32)

        def pv(n, p):
            return jnp.dot(vt_ref[0, n, rows, :], p.astype(BF16), preferred_element_type=F32)

        s = scores(i) + bown_ref[h]
        m = jnp.max(s, axis=0, keepdims=True)
        p = jnp.exp(s - m)
        m_ref[...] = m
        l_ref[...] = jnp.sum(p, axis=0, keepdims=True)
        acc_ref[...] = pv(i, p)

        def update(n, s):
            m_old = m_ref[...]
            m_new = jnp.maximum(m_old, jnp.max(s, axis=0, keepdims=True))
            a = jnp.exp(m_old - m_new)
            p = jnp.exp(s - m_new)
            l_ref[...] = a * l_ref[...] + jnp.sum(p, axis=0, keepdims=True)
            acc_ref[...] = a * acc_ref[...] + pv(n, p)
            m_ref[...] = m_new

        def far(n, c):
            update(n, scores(n) + (msk_ref[pl.ds(n, 1), :] + cfar))
            return c

        lax.fori_loop(0, jnp.maximum(i - 1, 0), far, 0)

        @pl.when(i >= 1)
        def _():
            n = i - 1
            update(n, scores(n) + bprev_ref[h] + msk_ref[pl.ds(n, 1), :])

        o = acc_ref[...] / l_ref[...]
        yb_ref[:, rows] = o.T

    o_ref[0] = _merge(yb_ref[...], gbs_ref[0], hac_ref[0], smb_ref[0], x_ref[0], pb_ref, wout_ref, lng_ref,
                      lnb_ref, alpha)


def _attn_prompt(cfar, qt, kb, vt, kmean, bown, bprev, x, hac, smb, gbs, pb, wout, lng, lnb, alpha):
    bsz, t, d_model = x.shape
    nb, tq = kb.shape[1], kb.shape[2]
    d_attn = kb.shape[3]
    kern = functools.partial(_attn_prompt_kernel, nb=nb, tq=tq, alpha=alpha)
    row = lambda w: pl.BlockSpec((1, tq, w), lambda b, i: (b, i, 0))
    per_b = lambda a: pl.BlockSpec((1,) + a.shape[1:], lambda b, i: (b,) + (0,) * (a.ndim - 1),
                                   pipeline_mode=pl.Buffered(1))
    in_specs = [
        pl.BlockSpec(memory_space=pltpu.SMEM),
        pl.BlockSpec((1, 1, d_attn, tq), lambda b, i: (b, i, 0, 0)),
        per_b(kb), per_b(vt), per_b(kmean),
        _const_spec(bown.shape), _const_spec(bprev.shape),
        row(d_model), row(d_model), row(d_model), row(d_attn),
        _const_spec(pb.shape), _const_spec(wout.shape), _const_spec(lng.shape), _const_spec(lnb.shape),
    ]
    return pl.pallas_call(
        kern, grid=(bsz, nb), in_specs=in_specs, out_specs=row(d_model),
        out_shape=jax.ShapeDtypeStruct((bsz, t, d_model), F32),
        scratch_shapes=[pltpu.VMEM((nb, tq), F32), pltpu.VMEM((1, tq), F32), pltpu.VMEM((1, tq), F32),
                        pltpu.VMEM((HEAD_DIM, tq), F32), pltpu.VMEM((tq, d_attn), F32)],
        compiler_params=pltpu.CompilerParams(dimension_semantics=("arbitrary", "arbitrary"),
                                             vmem_limit_bytes=VMEM_LIMIT),
        name="attn_prompt",
    )(cfar, qt, kb, vt, kmean, bown, bprev, x, hac, smb, gbs, pb, wout, lng, lnb)


def _proj_sample_kernel(x_ref, pst_in_ref, cst_in_ref, win_ref, wpool_ref, spool_ref, wdw_ref, bdw_ref, clng_ref,
                        clnb_ref, wpw2_ref, pa_ref, pc_ref,
                        q_ref, k_ref, v_ref, gbs_ref, smb_ref, hac_ref, pst_ref, cst_ref,
                        extu_ref, exta_ref, *, tm, rs, sp, d_pool):
    xb = x_ref[...].astype(BF16)

    def seg(name):
        a, b = sp[name]
        return jnp.dot(xb, win_ref[:, a:b], preferred_element_type=F32)

    n_pool, n_conv = (POOL_HALO - 1) * rs, (CONV_W - 1) * rs
    extu_ref[0:rs, :] = jnp.zeros((rs, d_pool), F32)
    extu_ref[pl.ds(rs, n_pool), :] = pst_in_ref[...]
    exta_ref[0:(CONV_HALO - CONV_W + 1) * rs, :] = jnp.zeros(((CONV_HALO - CONV_W + 1) * rs, exta_ref.shape[1]), F32)
    exta_ref[pl.ds((CONV_HALO - CONV_W + 1) * rs, n_conv), :] = cst_in_ref[...]

    wl = jnp.zeros((1, d_pool), jnp.int32)
    lane = lax.broadcasted_iota(jnp.int32, (1, d_pool), 1)
    gdim = d_pool // len(POOL_WINDOWS)
    for g, w in enumerate(POOL_WINDOWS):
        wl = jnp.where(lane >= g * gdim, w, wl)
    cnt = wl.astype(F32)

    hac_ref[...] = _branches(seg, extu_ref, exta_ref, cnt, wpool_ref, spool_ref, wdw_ref, bdw_ref, clng_ref,
                             clnb_ref, wpw2_ref, pa_ref, pc_ref, tm=tm, rs=rs, d_pool=d_pool)
    pst_ref[...] = extu_ref[pl.ds(POOL_HALO * rs + tm - n_pool, n_pool), :]
    cst_ref[...] = exta_ref[pl.ds(CONV_HALO * rs + tm - n_conv, n_conv), :]
    q_ref[...] = seg("q")
    k_ref[...] = seg("k")
    v_ref[...] = seg("v")
    gbs_ref[...] = _silu(seg("g_b"))
    smb_ref[...] = _sigmoid(seg("m_b"))


def _proj_sample(x_tm, pst_tm, cst_tm, win, wpool, spool, wdw, bdw, clng, clnb, wpw2, pa, pc, sp, rs):
    tm, d_model = x_tm.shape
    d_pool, d_conv, d_attn = wpool.shape[0], wpw2.shape[0], sp["q"][1] - sp["q"][0]
    kern = functools.partial(_proj_sample_kernel, tm=tm, rs=rs, sp=sp, d_pool=d_pool)
    out_shape = (
        jax.ShapeDtypeStruct((tm, d_attn), F32), jax.ShapeDtypeStruct((tm, d_attn), F32),
        jax.ShapeDtypeStruct((tm, d_attn), F32), jax.ShapeDtypeStruct((tm, d_attn), F32),
        jax.ShapeDtypeStruct((tm, d_model), F32), jax.ShapeDtypeStruct((tm, d_model), F32),
        jax.ShapeDtypeStruct(pst_tm.shape, F32), jax.ShapeDtypeStruct(cst_tm.shape, F32),
    )
    return pl.pallas_call(
        kern, out_shape=out_shape,
        scratch_shapes=[pltpu.VMEM((POOL_HALO * rs + tm, d_pool), F32), pltpu.VMEM((CONV_HALO * rs + tm, d_conv), F32)],
        compiler_params=pltpu.CompilerParams(vmem_limit_bytes=VMEM_LIMIT),
        name="proj_sample",
    )(x_tm, pst_tm, cst_tm, win, wpool, spool, wdw, bdw, clng, clnb, wpw2, pa, pc)


def _attn_sample_kernel(pt_ref, k0_ref, k1_ref, v0_ref, v1_ref, qbd_ref, knew_ref, vnew_ref, cfar_ref, blast_ref,
                        bown_ref, bd_ref, yb_ref, ms_ref, ls_ref, gs_ref, r_ref, *, nb, tdec):
    j = pl.program_id(1)
    qbd = qbd_ref[0]
    bd = bd_ref[...]
    nrow = qbd.shape[0]
    qs = (qbd * (HEAD_DIM ** -0.5)).astype(BF16)
    k0, k1 = k0_ref[0, 0], k1_ref[0, 0]
    nt = (((1,), (1,)), ((), ()))
    s = jnp.concatenate([lax.dot_general(qs, k0.astype(BF16), nt, preferred_element_type=F32),
                         lax.dot_general(qs, k1.astype(BF16), nt, preferred_element_type=F32)], axis=1)
    s = s + jnp.where(j == nb - 1, blast_ref[...], cfar_ref[...])
    m = jnp.max(s, axis=1, keepdims=True)
    p = jnp.exp(s - m)
    l = jnp.sum(p, axis=1, keepdims=True)
    pb = p.astype(BF16)
    half = pb.shape[1] // 2
    r = (jnp.dot(pb[:, :half], v0_ref[0, 0].astype(BF16), preferred_element_type=F32)
         + jnp.dot(pb[:, half:], v1_ref[0, 0].astype(BF16), preferred_element_type=F32))
    kmean = (jnp.sum(k0, axis=0, keepdims=True) + jnp.sum(k1, axis=0, keepdims=True)) * (1.0 / (2 * k0.shape[0]))
    gate = jnp.sum(qbd * kmean, axis=1, keepdims=True)
    ms_ref[j] = jnp.broadcast_to(m, (nrow, LANES))
    ls_ref[j] = jnp.broadcast_to(l, (nrow, LANES))
    gs_ref[j] = jnp.broadcast_to(gate, (nrow, LANES))
    r_ref[j] = r * bd

    @pl.when(j == nb - 1)
    def _():
        knew, vnew = knew_ref[0], vnew_ref[0]
        qsf = qbd * (HEAD_DIM ** -0.5)
        cols = [jnp.sum(qsf * knew[t:t + 1, :], axis=1, keepdims=True) + bown_ref[:, t:t + 1] for t in range(tdec)]
        m_o = cols[0]
        for c in cols[1:]:
            m_o = jnp.maximum(m_o, c)
        g = gs_ref[...]
        n_iota = lax.broadcasted_iota(jnp.int32, g.shape, 0)
        sel = jnp.zeros(g.shape, jnp.bool_)
        for _ in range(min(MOBA_TOPK, nb)):
            mx = jnp.max(g, axis=0, keepdims=True)
            idx = jnp.min(jnp.where(g == mx, n_iota, nb), axis=0, keepdims=True)
            pick = n_iota == idx
            sel = jnp.logical_or(sel, pick)
            g = jnp.where(pick, -jnp.inf, g)
        ms = jnp.where(sel, ms_ref[...], NEG)
        mtot = jnp.maximum(jnp.max(ms, axis=0), jnp.broadcast_to(m_o, (nrow, LANES)))
        w = jnp.where(sel, jnp.exp(ms - mtot[None]), 0.0)
        ltot = jnp.sum(w * ls_ref[...], axis=0)[:, :1]
        mcol = mtot[:, :1]
        acc = jnp.zeros(bd.shape, F32)
        for t in range(tdec):
            pt_ = jnp.exp(cols[t] - mcol)
            ltot = ltot + pt_
            acc = acc + pt_ * vnew[t:t + 1, :]
        acc = acc * bd
        reps = bd.shape[1] // LANES
        for n in range(nb):
            acc = acc + jnp.concatenate([w[n]] * reps, axis=1) * r_ref[n]
        out = acc / ltot
        yb_ref[0] = jnp.sum(out.reshape(tdec, N_HEADS, bd.shape[1]), axis=1)


def _attn_sample(page_table, cache_k, cache_v, layer, qbd, knew, vnew, cfar, blast, bown, bd):
    dec_b, nrow, d_attn = qbd.shape
    tdec = knew.shape[1]
    page = cache_k.shape[2]
    ppb = MOBA_BLOCK // page
    assert ppb == 2, "one MoBA block is swept as two cache pages"
    nb = page_table.shape[1] // ppb
    kern = functools.partial(_attn_sample_kernel, nb=nb, tdec=tdec)
    pg = lambda o: pl.BlockSpec((1, 1, page, d_attn), lambda b, j, pt: (pt[b, ppb * j + o], layer, 0, 0))
    full = lambda a: pl.BlockSpec(a.shape, lambda b, j, pt: (0,) * a.ndim)
    per_b = lambda a: pl.BlockSpec((1,) + a.shape[1:], lambda b, j, pt: (b,) + (0,) * (a.ndim - 1))
    grid_spec = pltpu.PrefetchScalarGridSpec(
        num_scalar_prefetch=1, grid=(dec_b, nb),
        in_specs=[pg(0), pg(1), pg(0), pg(1), per_b(qbd), per_b(knew), per_b(vnew), full(cfar), full(blast),
                  full(bown), full(bd)],
        out_specs=pl.BlockSpec((1, tdec, d_attn), lambda b, j, pt: (b, 0, 0)),
        scratch_shapes=[pltpu.VMEM((nb, nrow, LANES), F32), pltpu.VMEM((nb, nrow, LANES), F32),
                        pltpu.VMEM((nb, nrow, LANES), F32), pltpu.VMEM((nb, nrow, d_attn), F32)],
    )
    return pl.pallas_call(
        kern, grid_spec=grid_spec, out_shape=jax.ShapeDtypeStruct((dec_b, tdec, d_attn), F32),
        compiler_params=pltpu.CompilerParams(dimension_semantics=("arbitrary", "arbitrary"),
                                             vmem_limit_bytes=VMEM_LIMIT),
        name="attn_sample",
    )(page_table, cache_k, cache_k, cache_v, cache_v, qbd, knew, vnew, cfar, blast, bown, bd)


def _merge_sample_kernel(yb_ref, gbs_ref, hac_ref, smb_ref, x_ref, pb_ref, wout_ref, lng_ref, lnb_ref, o_ref, *, alpha):
    o_ref[...] = _merge(yb_ref[...], gbs_ref[...], hac_ref[...], smb_ref[...], x_ref[...], pb_ref, wout_ref,
                        lng_ref, lnb_ref, alpha)


def _merge_sample(yb, gbs, hac, smb, x, pb, wout, lng, lnb, alpha):
    return pl.pallas_call(
        functools.partial(_merge_sample_kernel, alpha=alpha),
        out_shape=jax.ShapeDtypeStruct(x.shape, F32),
        compiler_params=pltpu.CompilerParams(vmem_limit_bytes=VMEM_LIMIT),
        name="merge_sample",
    )(yb, gbs, hac, smb, x, pb, wout, lng, lnb)


def _block_diag(w):
    g, c, d = w.shape
    out = jnp.zeros((g * c, g * d), w.dtype)
    for i in range(g):
        out = out.at[i * c:(i + 1) * c, i * d:(i + 1) * d].set(w[i])
    return out


def kernel(x_prompt, x_sample, cache_k, cache_v, page_table, state_pool, state_conv, rel_bias, w_in, w_pool, s_pool,
           w_dw, b_dw, conv_ln_g, conv_ln_b, w_pw2, p_a, p_b, p_c, w_out, ln_g, ln_b):
    depth = w_in.shape[0]
    bsz, t, d_model = x_prompt.shape
    dec_b, tdec, _ = x_sample.shape
    d_pool = w_pool.shape[1] * w_pool.shape[2]
    d_attn, d_conv = p_b.shape[1], p_c.shape[1]
    assert d_attn == N_HEADS * HEAD_DIM and t % MOBA_BLOCK == 0
    n_pool_pages, _, page, _, _ = cache_k.shape
    past_len = page_table.shape[1] * page
    assert past_len % MOBA_BLOCK == 0 and past_len >= POOL_HALO
    nb_s = past_len // MOBA_BLOCK
    sp = _splits(d_pool, d_attn, d_conv, d_model)
    alpha = (2 * depth) ** 0.25
    n_buckets = rel_bias.shape[0]

    bucket = _bucket_table(n_buckets, 2 * MOBA_BLOCK + tdec)
    far_rel = int(np.argmax(bucket == bucket[-1]))
    assert (bucket[far_rel:] == bucket[-1]).all() and far_rel <= MOBA_BLOCK
    tab = rel_bias.T.astype(F32)[:, bucket]
    cfar_h = tab[:, -1]
    qi = np.arange(MOBA_BLOCK)[None, :]
    kj = np.arange(MOBA_BLOCK)[:, None]
    d_own = qi - kj
    bown_p = jnp.where(d_own >= 0, tab[:, np.maximum(d_own, 0)], NEG)
    bprev_p = tab[:, d_own + MOBA_BLOCK]
    row_q = np.repeat(np.arange(tdec), N_HEADS)
    row_h = np.tile(np.arange(N_HEADS), tdec)
    kr = np.arange(MOBA_BLOCK)[None, :]
    blast_s = tab[row_h[:, None], MOBA_BLOCK + row_q[:, None] - kr]
    d_new = row_q[:, None] - np.arange(tdec)[None, :]
    bown_s = jnp.where(d_new >= 0, tab[row_h[:, None], np.maximum(d_new, 0)], NEG)
    cfar_s = cfar_h[row_h][:, None]
    bd_s = jnp.asarray((np.arange(d_attn)[None, :] // HEAD_DIM == row_h[:, None]).astype(np.float32))

    cache_k4 = cache_k.reshape(n_pool_pages, depth, page, d_attn)
    cache_v4 = cache_v.reshape(n_pool_pages, depth, page, d_attn)

    xp = x_prompt
    xs = x_sample.transpose(1, 0, 2).reshape(tdec * dec_b, d_model)
    outs = {n: [] for n in ("kp", "vp", "ks", "vs", "pp", "ps", "cp", "cs")}
    for l in range(depth):
        win = w_in[l].astype(BF16)
        wpool = _block_diag(w_pool[l]).astype(BF16)
        wl = (win, wpool, s_pool[l][None], w_dw[l], b_dw[l][None], conv_ln_g[l][None], conv_ln_b[l][None],
              w_pw2[l].astype(BF16), p_a[l].astype(BF16), p_c[l].astype(BF16))
        pb, wout, lng, lnb = p_b[l].astype(BF16), w_out[l].astype(BF16), ln_g[l][None], ln_b[l][None]

        k, v, kb, vt, qt, kmean, gbs, smb, hac, pst, cst = _proj_prompt(xp, *wl, sp)
        xp = _attn_prompt(cfar_h, qt, kb, vt, kmean.reshape(bsz, -1, d_attn), bown_p, bprev_p, xp, hac, smb, gbs,
                          pb, wout, lng, lnb, alpha)
        outs["kp"].append(k.reshape(bsz, t, N_HEADS, HEAD_DIM))
        outs["vp"].append(v.reshape(bsz, t, N_HEADS, HEAD_DIM))
        outs["pp"].append(pst)
        outs["cp"].append(cst)

        pst_tm = state_pool[:, l].transpose(1, 0, 2).reshape(-1, d_pool)
        cst_tm = state_conv[:, l].transpose(1, 0, 2).reshape(-1, d_conv)
        q_s, k_s, v_s, gbs_s, smb_s, hac_s, pst_s, cst_s = _proj_sample(xs, pst_tm, cst_tm, *wl, sp, dec_b)
        to_b = lambda a: a.reshape(tdec, dec_b, -1).transpose(1, 0, 2)
        q_b, k_b, v_b = to_b(q_s), to_b(k_s), to_b(v_s)
        qbd = (q_b[:, :, None, :] * bd_s.reshape(tdec, N_HEADS, d_attn)[None]).reshape(dec_b, tdec * N_HEADS, d_attn)
        yb_s = _attn_sample(page_table, cache_k4, cache_v4, l, qbd, k_b, v_b, cfar_s, blast_s, bown_s, bd_s)
        yb_tm = yb_s.transpose(1, 0, 2).reshape(tdec * dec_b, d_attn)
        xs = _merge_sample(yb_tm, gbs_s, hac_s, smb_s, xs, pb, wout, lng, lnb, alpha)
        outs["ks"].append(k_b.reshape(dec_b, tdec, N_HEADS, HEAD_DIM))
        outs["vs"].append(v_b.reshape(dec_b, tdec, N_HEADS, HEAD_DIM))
        outs["ps"].append(pst_s.reshape(-1, dec_b, d_pool).transpose(1, 0, 2))
        outs["cs"].append(cst_s.reshape(-1, dec_b, d_conv).transpose(1, 0, 2))

    st = lambda n: jnp.stack(outs[n], axis=1)
    y_sample = xs.reshape(tdec, dec_b, d_model).transpose(1, 0, 2)
    return (xp, y_sample, st("kp"), st("vp"), st("ks"), st("vs"), st("pp"), st("ps"), st("cp"), st("cs"))
```

```python
import functools
import math

import numpy as np
import jax
import jax.numpy as jnp
from jax import lax
from jax.experimental import pallas as pl
from jax.experimental.pallas import tpu as pltpu

POOL_WINDOWS = (2, 4, 8, 16)
POOL_HALO = 16
CONV_W = 31
CONV_HALO = 32
N_HEADS = 8
HEAD_DIM = 64
MOBA_BLOCK = 256
MOBA_TOPK = 3
REL_MAX_DIST = 128
LN_EPS = 1e-5
NEG = -1e30
LOG2E = 1.4426950408889634
LANES = 128
VMEM_LIMIT = 56 * 1024 * 1024
SAMPLE_BLOCKS_PER_STEP = 4
FAR_UNROLL = 4

F32 = jnp.float32
BF16 = jnp.bfloat16


def _sigmoid(x):
    return 1.0 / (1.0 + jnp.exp(-x))


def _silu(x):
    return x * _sigmoid(x)


def _layer_norm(x, g, b):
    mu = jnp.mean(x, axis=-1, keepdims=True)
    xc = x - mu
    var = jnp.mean(xc * xc, axis=-1, keepdims=True)
    return xc * lax.rsqrt(var + LN_EPS) * g + b


def _bucket_table(n_buckets, n):
    rel = np.arange(n)
    max_exact = n_buckets // 2
    nf = np.maximum(rel, 1).astype(np.float32)
    large = max_exact + (np.log(nf / np.float32(max_exact)) / np.float32(math.log(REL_MAX_DIST / max_exact))
                         * np.float32(n_buckets - max_exact)).astype(np.int32)
    large = np.minimum(large, n_buckets - 1)
    return np.where(rel < max_exact, rel, large).astype(np.int32)


def _splits(d_pool, d_attn, d_conv, d_model):
    names = ("u_a", "g_a", "q", "k", "v", "g_b", "c_val", "c_gate", "g_c", "m_a", "m_b", "m_c")
    widths = (d_pool, d_pool, d_attn, d_attn, d_attn, d_attn, d_conv, d_conv, d_conv, d_model, d_model, d_model)
    off, out = 0, {}
    for nme, w in zip(names, widths):
        out[nme] = (off, off + w)
        off += w
    return out


def _top_k_mask(g, n_iota, k, n):
    sel = jnp.zeros(g.shape, jnp.bool_)
    for _ in range(k):
        mx = jnp.max(g, axis=0, keepdims=True)
        idx = jnp.min(jnp.where(g == mx, n_iota, n), axis=0, keepdims=True)
        pick = n_iota == idx
        sel = jnp.logical_or(sel, pick)
        g = jnp.where(pick, -jnp.inf, g)
    return sel


def _branches(seg, extu_ref, exta_ref, cnt, wpool_ref, spool_ref, wdw_ref, bdw_ref, clng_ref, clnb_ref,
              wpw2_ref, pa_ref, pc_ref, *, tm, rs, d_pool):
    u = seg("u_a")
    extu_ref[pl.ds(POOL_HALO * rs, tm), :] = u
    gdim = d_pool // len(POOL_WINDOWS)
    lane = lax.broadcasted_iota(jnp.int32, (1, d_pool), 1)
    acc = u
    num = None
    j = 1
    for g, w in enumerate(POOL_WINDOWS):
        while j < w:
            acc = acc + extu_ref[pl.ds((POOL_HALO - j) * rs, tm), :]
            j += 1
        num = acc if num is None else jnp.where(lane >= g * gdim, acc, num)
    pooled = num / cnt
    r = pooled - u
    ya = jnp.dot(r.astype(BF16), wpool_ref[...], preferred_element_type=F32) * spool_ref[...]
    ya = ya * _silu(seg("g_a"))
    hac = _sigmoid(seg("m_a")) * jnp.dot(ya.astype(BF16), pa_ref[...], preferred_element_type=F32)

    a = seg("c_val") * _sigmoid(seg("c_gate"))
    exta_ref[pl.ds(CONV_HALO * rs, tm), :] = a
    h = jnp.zeros_like(a) + bdw_ref[...]
    for j in range(CONV_W):
        h = h + wdw_ref[j:j + 1, :] * exta_ref[pl.ds((CONV_HALO - (CONV_W - 1) + j) * rs, tm), :]
    h = _silu(_layer_norm(h, clng_ref[...], clnb_ref[...]))
    yc = jnp.dot(h.astype(BF16), wpw2_ref[...], preferred_element_type=F32) * _silu(seg("g_c"))
    hac = hac + _sigmoid(seg("m_c")) * jnp.dot(yc.astype(BF16), pc_ref[...], preferred_element_type=F32)
    return hac


def _merge(yb, gbs, hac, smb, x, pb_ref, wout_ref, lng_ref, lnb_ref, alpha):
    ybg = (yb * gbs).astype(BF16)
    h = hac + smb * jnp.dot(ybg, pb_ref[...], preferred_element_type=F32)
    y = alpha * x + jnp.dot(h.astype(BF16), wout_ref[...], preferred_element_type=F32)
    return _layer_norm(y, lng_ref[...], lnb_ref[...])


def _proj_prompt_kernel(x_ref, win_ref, wpool_ref, spool_ref, wdw_ref, bdw_ref, clng_ref, clnb_ref, wpw2_ref,
                        pa_ref, pc_ref,
                        k_ref, v_ref, kb_ref, vt_ref, qt_ref, kmean_ref, gbs_ref, smb_ref, hac_ref, pst_ref, cst_ref,
                        extu_ref, exta_ref, *, tm, sp, d_pool, n_tiles):
    i = pl.program_id(1)
    xb = x_ref[0].astype(BF16)

    def seg(name):
        a, b = sp[name]
        return jnp.dot(xb, win_ref[:, a:b], preferred_element_type=F32)

    @pl.when(i == 0)
    def _():
        extu_ref[0:POOL_HALO, :] = jnp.zeros((POOL_HALO, d_pool), F32)
        exta_ref[0:CONV_HALO, :] = jnp.zeros((CONV_HALO, exta_ref.shape[1]), F32)

    pos1 = i * tm + lax.broadcasted_iota(jnp.int32, (tm, 1), 0) + 1
    wl = jnp.zeros((1, d_pool), jnp.int32)
    lane = lax.broadcasted_iota(jnp.int32, (1, d_pool), 1)
    gdim = d_pool // len(POOL_WINDOWS)
    for g, w in enumerate(POOL_WINDOWS):
        wl = jnp.where(lane >= g * gdim, w, wl)
    cnt = jnp.minimum(pos1, wl).astype(F32)

    hac_ref[0] = _branches(seg, extu_ref, exta_ref, cnt, wpool_ref, spool_ref, wdw_ref, bdw_ref, clng_ref,
                           clnb_ref, wpw2_ref, pa_ref, pc_ref, tm=tm, rs=1, d_pool=d_pool)

    @pl.when(i == n_tiles - 1)
    def _():
        pst_ref[0] = extu_ref[pl.ds(tm + 1, POOL_HALO - 1), :]
        cst_ref[0] = exta_ref[pl.ds(tm + CONV_HALO - (CONV_W - 1), CONV_W - 1), :]

    extu_ref[0:POOL_HALO, :] = extu_ref[pl.ds(tm, POOL_HALO), :]
    exta_ref[0:CONV_HALO, :] = exta_ref[pl.ds(tm, CONV_HALO), :]

    q = seg("q")
    qt_ref[0, 0] = q.T
    k = seg("k")
    k_ref[0] = k
    kb_ref[0, 0] = k.astype(BF16)
    kmean_ref[0, 0] = jnp.sum(k, axis=0, keepdims=True) * (1.0 / tm)
    v = seg("v")
    v_ref[0] = v
    vt_ref[0, 0] = v.T.astype(BF16)
    gbs_ref[0] = _silu(seg("g_b"))
    smb_ref[0] = _sigmoid(seg("m_b"))


def _const_spec(shape):
    nd = len(shape)
    return pl.BlockSpec(shape, lambda *_: (0,) * nd, pipeline_mode=pl.Buffered(1))


def _proj_prompt(x, win, wpool, spool, wdw, bdw, clng, clnb, wpw2, pa, pc, sp):
    bsz, t, d_model = x.shape
    tm = MOBA_BLOCK
    n_tiles = t // tm
    d_pool, d_conv, d_attn = wpool.shape[0], wpw2.shape[0], sp["q"][1] - sp["q"][0]
    kern = functools.partial(_proj_prompt_kernel, tm=tm, sp=sp, d_pool=d_pool, n_tiles=n_tiles)
    row = lambda w: pl.BlockSpec((1, tm, w), lambda b, i: (b, i, 0))
    blk = lambda r, c: pl.BlockSpec((1, 1, r, c), lambda b, i: (b, i, 0, 0))
    out_shape = (
        jax.ShapeDtypeStruct((bsz, t, d_attn), F32),
        jax.ShapeDtypeStruct((bsz, t, d_attn), F32),
        jax.ShapeDtypeStruct((bsz, n_tiles, tm, d_attn), BF16),
        jax.ShapeDtypeStruct((bsz, n_tiles, d_attn, tm), BF16),
        jax.ShapeDtypeStruct((bsz, n_tiles, d_attn, tm), F32),
        jax.ShapeDtypeStruct((bsz, n_tiles, 1, d_attn), F32),
        jax.ShapeDtypeStruct((bsz, t, d_attn), F32),
        jax.ShapeDtypeStruct((bsz, t, d_model), F32),
        jax.ShapeDtypeStruct((bsz, t, d_model), F32),
        jax.ShapeDtypeStruct((bsz, POOL_HALO - 1, d_pool), F32),
        jax.ShapeDtypeStruct((bsz, CONV_W - 1, d_conv), F32),
    )
    out_specs = (
        row(d_attn), row(d_attn), blk(tm, d_attn), blk(d_attn, tm), blk(d_attn, tm), blk(1, d_attn),
        row(d_attn), row(d_model), row(d_model),
        pl.BlockSpec((1, POOL_HALO - 1, d_pool), lambda b, i: (b, 0, 0)),
        pl.BlockSpec((1, CONV_W - 1, d_conv), lambda b, i: (b, 0, 0)),
    )
    in_specs = [row(d_model)] + [_const_spec(a.shape) for a in (win, wpool, spool, wdw, bdw, clng, clnb, wpw2, pa, pc)]
    return pl.pallas_call(
        kern, grid=(bsz, n_tiles), in_specs=in_specs, out_specs=out_specs, out_shape=out_shape,
        scratch_shapes=[pltpu.VMEM((POOL_HALO + tm, d_pool), F32), pltpu.VMEM((CONV_HALO + tm, d_conv), F32)],
        compiler_params=pltpu.CompilerParams(dimension_semantics=("arbitrary", "arbitrary"),
                                             vmem_limit_bytes=VMEM_LIMIT),
        name="proj_prompt",
    )(x, win, wpool, spool, wdw, bdw, clng, clnb, wpw2, pa, pc)


def _attn_prompt_kernel(rel_ref, qt_ref, kb_ref, vt_ref, kmean_ref, bko_ref, bkp_ref, x_ref, hac_ref, smb_ref,
                        gbs_ref, pb_ref, wout_ref, lng_ref, lnb_ref, o_ref,
                        bown_ref, bprev_ref, msk_ref, qb_ref, s_ref, p_ref, m_ref, l_ref, a_ref, acc_ref, yb_ref,
                        *, nb, tq, alpha, n_buckets):
    i = pl.program_id(1)

    @pl.when(jnp.logical_and(pl.program_id(0) == 0, i == 0))
    def _():
        for h in range(N_HEADS):
            bown_ref[h] = jnp.full((tq, tq), NEG, F32)
            bprev_ref[h] = jnp.zeros((tq, tq), F32)

            def fill(bkt, c):
                val = rel_ref[bkt, h] * LOG2E
                bown_ref[h] = jnp.where(bko_ref[...] == bkt, val, bown_ref[h])
                bprev_ref[h] = jnp.where(bkp_ref[...] == bkt, val, bprev_ref[h])
                return c

            lax.fori_loop(0, n_buckets, fill, 0)

    n_iota = lax.broadcasted_iota(jnp.int32, (nb, tq), 0)
    past = n_iota < i
    half = lax.broadcasted_iota(jnp.int32, (2 * HEAD_DIM, tq), 0) // HEAD_DIM
    pair = lambda h: slice((h // 2) * 2 * HEAD_DIM, (h // 2 + 1) * 2 * HEAD_DIM)
    rows = lambda h: slice(h * HEAD_DIM, (h + 1) * HEAD_DIM)

    for h in range(N_HEADS):
        qh = jnp.where(half == h % 2, qt_ref[0, 0, pair(h), :], 0.0)
        gate = jnp.dot(kmean_ref[0, :, pair(h)], qh, preferred_element_type=F32,
                       precision=lax.Precision.HIGHEST)
        sel = _top_k_mask(jnp.where(past, gate, NEG), n_iota, min(MOBA_TOPK, nb), nb)
        msk_ref[h] = jnp.where(jnp.logical_and(sel, past), 0.0, NEG)
        qb_ref[h] = (qh * (HEAD_DIM ** -0.5 * LOG2E)).astype(BF16)
        m_ref[h] = jnp.full((1, tq), NEG, F32)
        l_ref[h] = jnp.zeros((1, tq), F32)
        acc_ref[h] = jnp.zeros((HEAD_DIM, tq), F32)

    def scores(n, slot):
        for h in range(N_HEADS):
            s_ref[slot, h] = jnp.dot(kb_ref[0, n, :, pair(h)], qb_ref[h], preferred_element_type=F32)

    def softmax_pv(n, slot, bias_ref, shift_of):
        for h in range(N_HEADS):
            s = s_ref[slot, h] if bias_ref is None else s_ref[slot, h] + bias_ref[h]
            shift = shift_of(h)
            m_blk = jnp.max(s, axis=0, keepdims=True)
            m_old = m_ref[h]
            if shift is None:
                m_new = jnp.maximum(m_old, m_blk)
                p = jnp.exp2(s - m_new)
            else:
                m_new = jnp.maximum(m_old, m_blk + shift)
                p = jnp.exp2(s + (shift - m_new))
            a = jnp.exp2(m_old - m_new)
            l_ref[h] = a * l_ref[h] + jnp.sum(p, axis=0, keepdims=True)
            a_ref[h] = a
            m_ref[h] = m_new
            p_ref[h] = p.astype(BF16)
        for h in range(N_HEADS):
            acc_ref[h] = a_ref[h] * acc_ref[h] + jnp.dot(vt_ref[0, n, rows(h), :], p_ref[h],
                                                         preferred_element_type=F32)

    def far_shift(n):
        return lambda h: msk_ref[h, pl.ds(n, 1), :] + rel_ref[n_buckets - 1, h] * LOG2E

    def far_group(k, c):
        for u in range(FAR_UNROLL):
            scores(FAR_UNROLL * k + u, u)
        for u in range(FAR_UNROLL):
            softmax_pv(FAR_UNROLL * k + u, u, None, far_shift(FAR_UNROLL * k + u))
        return c

    def far_single(n, c):
        scores(n, 0)
        softmax_pv(n, 0, None, far_shift(n))
        return c

    n_far = jnp.maximum(i - 1, 0)
    n_grp = n_far // FAR_UNROLL
    lax.fori_loop(0, n_grp, far_group, 0)
    lax.fori_loop(n_grp * FAR_UNROLL, n_far, far_single, 0)

    @pl.when(i >= 1)
    def _():
        scores(i - 1, 0)
        softmax_pv(i - 1, 0, bprev_ref, lambda h: msk_ref[h, pl.ds(i - 1, 1), :])

    scores(i, 1)
    softmax_pv(i, 1, bown_ref, lambda h: None)

    for h in range(N_HEADS):
        o = acc_ref[h] / l_ref[h]
        yb_ref[:, rows(h)] = o.T

    o_ref[0] = _merge(yb_ref[...], gbs_ref[0], hac_ref[0], smb_ref[0], x_ref[0], pb_ref, wout_ref, lng_ref,
                      lnb_ref, alpha)


def _attn_prompt(rel_bias, qt, kb, vt, kmean, bko, bkp, x, hac, smb, gbs, pb, wout, lng, lnb, alpha):
    bsz, t, d_model = x.shape
    nb, tq = kb.shape[1], kb.shape[2]
    d_attn = kb.shape[3]
    kern = functools.partial(_attn_prompt_kernel, nb=nb, tq=tq, alpha=alpha, n_buckets=rel_bias.shape[0])
    row = lambda w: pl.BlockSpec((1, tq, w), lambda b, i: (b, i, 0))
    per_b = lambda a: pl.BlockSpec((1,) + a.shape[1:], lambda b, i: (b,) + (0,) * (a.ndim - 1),
                                   pipeline_mode=pl.Buffered(1))
    in_specs = [
        pl.BlockSpec(memory_space=pltpu.SMEM),
        pl.BlockSpec((1, 1, d_attn, tq), lambda b, i: (b, i, 0, 0)),
        per_b(kb), per_b(vt), per_b(kmean),
        _const_spec(bko.shape), _const_spec(bkp.shape),
        row(d_model), row(d_model), row(d_model), row(d_attn),
        _const_spec(pb.shape), _const_spec(wout.shape), _const_spec(lng.shape), _const_spec(lnb.shape),
    ]
    return pl.pallas_call(
        kern, grid=(bsz, nb), in_specs=in_specs, out_specs=row(d_model),
        out_shape=jax.ShapeDtypeStruct((bsz, t, d_model), F32),
        scratch_shapes=[pltpu.VMEM((N_HEADS, tq, tq), F32), pltpu.VMEM((N_HEADS, tq, tq), F32),
                        pltpu.VMEM((N_HEADS, nb, tq), F32), pltpu.VMEM((N_HEADS, 2 * HEAD_DIM, tq), BF16),
                        pltpu.VMEM((FAR_UNROLL, N_HEADS, tq, tq), F32), pltpu.VMEM((N_HEADS, tq, tq), BF16),
                        pltpu.VMEM((N_HEADS, 1, tq), F32), pltpu.VMEM((N_HEADS, 1, tq), F32),
                        pltpu.VMEM((N_HEADS, 1, tq), F32),
                        pltpu.VMEM((N_HEADS, HEAD_DIM, tq), F32), pltpu.VMEM((tq, d_attn), F32)],
        compiler_params=pltpu.CompilerParams(dimension_semantics=("arbitrary", "arbitrary"),
                                             vmem_limit_bytes=VMEM_LIMIT),
        name="attn_prompt",
    )(rel_bias, qt, kb, vt, kmean, bko, bkp, x, hac, smb, gbs, pb, wout, lng, lnb)


def _proj_sample_kernel(x_ref, pst_in_ref, cst_in_ref, win_ref, wpool_ref, spool_ref, wdw_ref, bdw_ref, clng_ref,
                        clnb_ref, wpw2_ref, pa_ref, pc_ref,
                        q_ref, k_ref, v_ref, gbs_ref, smb_ref, hac_ref, pst_ref, cst_ref,
                        extu_ref, exta_ref, *, tm, rs, sp, d_pool):
    xb = x_ref[...].astype(BF16)

    def seg(name):
        a, b = sp[name]
        return jnp.dot(xb, win_ref[:, a:b], preferred_element_type=F32)

    n_pool, n_conv = (POOL_HALO - 1) * rs, (CONV_W - 1) * rs
    extu_ref[0:rs, :] = jnp.zeros((rs, d_pool), F32)
    extu_ref[pl.ds(rs, n_pool), :] = pst_in_ref[...]
    exta_ref[0:(CONV_HALO - CONV_W + 1) * rs, :] = jnp.zeros(((CONV_HALO - CONV_W + 1) * rs, exta_ref.shape[1]), F32)
    exta_ref[pl.ds((CONV_HALO - CONV_W + 1) * rs, n_conv), :] = cst_in_ref[...]

    wl = jnp.zeros((1, d_pool), jnp.int32)
    lane = lax.broadcasted_iota(jnp.int32, (1, d_pool), 1)
    gdim = d_pool // len(POOL_WINDOWS)
    for g, w in enumerate(POOL_WINDOWS):
        wl = jnp.where(lane >= g * gdim, w, wl)
    cnt = wl.astype(F32)

    hac_ref[...] = _branches(seg, extu_ref, exta_ref, cnt, wpool_ref, spool_ref, wdw_ref, bdw_ref, clng_ref,
                             clnb_ref, wpw2_ref, pa_ref, pc_ref, tm=tm, rs=rs, d_pool=d_pool)
    pst_ref[...] = extu_ref[pl.ds(POOL_HALO * rs + tm - n_pool, n_pool), :]
    cst_ref[...] = exta_ref[pl.ds(CONV_HALO * rs + tm - n_conv, n_conv), :]
    q_ref[...] = seg("q")
    k_ref[...] = seg("k")
    v_ref[...] = seg("v")
    gbs_ref[...] = _silu(seg("g_b"))
    smb_ref[...] = _sigmoid(seg("m_b"))


def _proj_sample(x_tm, pst_tm, cst_tm, win, wpool, spool, wdw, bdw, clng, clnb, wpw2, pa, pc, sp, rs):
    tm, d_model = x_tm.shape
    d_pool, d_conv, d_attn = wpool.shape[0], wpw2.shape[0], sp["q"][1] - sp["q"][0]
    kern = functools.partial(_proj_sample_kernel, tm=tm, rs=rs, sp=sp, d_pool=d_pool)
    out_shape = (
        jax.ShapeDtypeStruct((tm, d_attn), F32), jax.ShapeDtypeStruct((tm, d_attn), F32),
        jax.ShapeDtypeStruct((tm, d_attn), F32), jax.ShapeDtypeStruct((tm, d_attn), F32),
        jax.ShapeDtypeStruct((tm, d_model), F32), jax.ShapeDtypeStruct((tm, d_model), F32),
        jax.ShapeDtypeStruct(pst_tm.shape, F32), jax.ShapeDtypeStruct(cst_tm.shape, F32),
    )
    return pl.pallas_call(
        kern, out_shape=out_shape,
        scratch_shapes=[pltpu.VMEM((POOL_HALO * rs + tm, d_pool), F32), pltpu.VMEM((CONV_HALO * rs + tm, d_conv), F32)],
        compiler_params=pltpu.CompilerParams(vmem_limit_bytes=VMEM_LIMIT),
        name="proj_sample",
    )(x_tm, pst_tm, cst_tm, win, wpool, spool, wdw, bdw, clng, clnb, wpw2, pa, pc)


def _attn_sample_kernel(pt_ref, *refs, nb, bps, tdec, n_buckets):
    npg = 2 * bps
    k_refs, v_refs = refs[:npg], refs[npg:2 * npg]
    (q_ref, knew_ref, vnew_ref, relt_ref, bks_ref, bkn_ref, o_ref,
     blast_ref, cfarm_ref, bown_ref, ms_ref, ls_ref, gs_ref, r_ref) = refs[2 * npg:]
    jj = pl.program_id(1)
    nrow = tdec * N_HEADS
    nt = (((1,), (1,)), ((), ()))

    @pl.when(jnp.logical_and(pl.program_id(0) == 0, jj == 0))
    def _():
        bks, bkn = bks_ref[...], bkn_ref[...]
        tl = jnp.full(bks.shape, NEG, F32)
        tn = jnp.full(bkn.shape, NEG, F32)
        for bkt in range(n_buckets):
            col = relt_ref[:, bkt:bkt + 1]
            tl = jnp.where(bks == bkt, col, tl)
            tn = jnp.where(bkn == bkt, col, tn)
        blast_ref[...] = tl
        bown_ref[...] = tn
        cfarm_ref[...] = jnp.where(bks >= 0, relt_ref[:, n_buckets - 1:n_buckets], NEG)

    qf = q_ref[0]
    qs = (qf * (HEAD_DIM ** -0.5)).astype(BF16)
    flat = lambda ref: ref[0, 0].reshape(-1, HEAD_DIM)

    def partial_softmax(s, vs):
        m = jnp.max(s, axis=1, keepdims=True)
        p = jnp.exp(s - m)
        l = jnp.sum(p, axis=1, keepdims=True)
        pb = p.astype(BF16)
        r, off = None, 0
        for v in vs:
            d = jnp.dot(pb[:, off:off + v.shape[0]], v, preferred_element_type=F32)
            r = d if r is None else r + d
            off += v.shape[0]
        return m, l, r

    for c in range(bps):
        j = jj * bps + c
        k0, k1 = flat(k_refs[2 * c]), flat(k_refs[2 * c + 1])
        s = jnp.concatenate([lax.dot_general(qs, k0.astype(BF16), nt, preferred_element_type=F32),
                             lax.dot_general(qs, k1.astype(BF16), nt, preferred_element_type=F32)], axis=1)
        if c == bps - 1:
            s = s + jnp.where(j == nb - 1, blast_ref[...], cfarm_ref[...])
        else:
            s = s + cfarm_ref[...]
        m, l, r = partial_softmax(s, [flat(v_refs[2 * c]).astype(BF16), flat(v_refs[2 * c + 1]).astype(BF16)])
        ksum = (jnp.sum(k_refs[2 * c][0, 0], axis=0) + jnp.sum(k_refs[2 * c + 1][0, 0], axis=0))
        kmean = ksum * (1.0 / MOBA_BLOCK)
        gate = jnp.concatenate(
            [jnp.sum(qf[t * N_HEADS:(t + 1) * N_HEADS] * kmean, axis=1, keepdims=True) for t in range(tdec)], axis=0)
        ms_ref[j] = m
        ls_ref[j] = l
        gs_ref[j] = gate
        r_ref[j] = r

    @pl.when(jj == nb // bps - 1)
    def _():
        kn = knew_ref[0].reshape(-1, HEAD_DIM)
        s_o = lax.dot_general(qs, kn.astype(BF16), nt, preferred_element_type=F32) + bown_ref[...]
        m_o, l_o, r_o = partial_softmax(s_o, [vnew_ref[0].reshape(-1, HEAD_DIM).astype(BF16)])
        g = gs_ref[...]
        n_iota = lax.broadcasted_iota(jnp.int32, g.shape, 0)
        sel = _top_k_mask(g, n_iota, min(MOBA_TOPK, nb), nb)
        ms = jnp.where(sel, ms_ref[...], NEG)
        mtot = jnp.maximum(jnp.max(ms, axis=0), m_o)
        w = jnp.where(sel, jnp.exp(ms - mtot[None]), 0.0)
        w_o = jnp.exp(m_o - mtot)
        ltot = w_o * l_o + jnp.sum(w * ls_ref[...], axis=0)
        acc = w_o * r_o + jnp.sum(w * r_ref[...], axis=0)
        o_ref[0] = acc / ltot


def _attn_sample(page_table, cache_k, cache_v, layer, qf, knew, vnew, relt, bks, bkn):
    dec_b, nrow, _ = qf.shape
    tdec = knew.shape[1]
    page = cache_k.shape[2]
    assert 2 * page == MOBA_BLOCK, "one MoBA block is swept as two cache pages"
    nb = page_table.shape[1] // 2
    bps = math.gcd(SAMPLE_BLOCKS_PER_STEP, nb)
    kern = functools.partial(_attn_sample_kernel, nb=nb, bps=bps, tdec=tdec, n_buckets=relt.shape[1])
    pg = lambda o: pl.BlockSpec((1, 1, page, N_HEADS, HEAD_DIM),
                                lambda b, jj, pt: (pt[b, 2 * bps * jj + o], layer, 0, 0, 0))
    full = lambda a: pl.BlockSpec(a.shape, lambda b, jj, pt: (0,) * a.ndim)
    per_b = lambda a: pl.BlockSpec((1,) + a.shape[1:], lambda b, jj, pt: (b,) + (0,) * (a.ndim - 1))
    grid_spec = pltpu.PrefetchScalarGridSpec(
        num_scalar_prefetch=1, grid=(dec_b, nb // bps),
        in_specs=[pg(o) for o in range(2 * bps)] * 2 + [per_b(qf), per_b(knew), per_b(vnew), full(relt), full(bks),
                                                        full(bkn)],
        out_specs=pl.BlockSpec((1, nrow, HEAD_DIM), lambda b, jj, pt: (b, 0, 0)),
        scratch_shapes=[pltpu.VMEM(bks.shape, F32), pltpu.VMEM(bks.shape, F32), pltpu.VMEM(bkn.shape, F32),
                        pltpu.VMEM((nb, nrow, 1), F32), pltpu.VMEM((nb, nrow, 1), F32),
                        pltpu.VMEM((nb, nrow, 1), F32), pltpu.VMEM((nb, nrow, HEAD_DIM), F32)],
    )
    return pl.pallas_call(
        kern, grid_spec=grid_spec, out_shape=jax.ShapeDtypeStruct((dec_b, nrow, HEAD_DIM), F32),
        compiler_params=pltpu.CompilerParams(dimension_semantics=("arbitrary", "arbitrary"),
                                             vmem_limit_bytes=VMEM_LIMIT),
        name="attn_sample",
    )(page_table, *([cache_k] * (2 * bps)), *([cache_v] * (2 * bps)), qf, knew, vnew, relt, bks, bkn)


def _merge_sample_kernel(yb_ref, gbs_ref, hac_ref, smb_ref, x_ref, pb_ref, wout_ref, lng_ref, lnb_ref, o_ref, *, alpha):
    o_ref[...] = _merge(yb_ref[...], gbs_ref[...], hac_ref[...], smb_ref[...], x_ref[...], pb_ref, wout_ref,
                        lng_ref, lnb_ref, alpha)


def _merge_sample(yb, gbs, hac, smb, x, pb, wout, lng, lnb, alpha):
    return pl.pallas_call(
        functools.partial(_merge_sample_kernel, alpha=alpha),
        out_shape=jax.ShapeDtypeStruct(x.shape, F32),
        compiler_params=pltpu.CompilerParams(vmem_limit_bytes=VMEM_LIMIT),
        name="merge_sample",
    )(yb, gbs, hac, smb, x, pb, wout, lng, lnb)


def _block_diag(w):
    g, c, d = w.shape
    out = jnp.zeros((g * c, g * d), w.dtype)
    for i in range(g):
        out = out.at[i * c:(i + 1) * c, i * d:(i + 1) * d].set(w[i])
    return out


def kernel(x_prompt, x_sample, cache_k, cache_v, page_table, state_pool, state_conv, rel_bias, w_in, w_pool, s_pool,
           w_dw, b_dw, conv_ln_g, conv_ln_b, w_pw2, p_a, p_b, p_c, w_out, ln_g, ln_b):
    depth = w_in.shape[0]
    bsz, t, d_model = x_prompt.shape
    dec_b, tdec, _ = x_sample.shape
    d_pool = w_pool.shape[1] * w_pool.shape[2]
    d_attn, d_conv = p_b.shape[1], p_c.shape[1]
    assert d_attn == N_HEADS * HEAD_DIM and t % MOBA_BLOCK == 0
    page = cache_k.shape[2]
    past_len = page_table.shape[1] * page
    assert past_len % MOBA_BLOCK == 0 and past_len >= POOL_HALO
    sp = _splits(d_pool, d_attn, d_conv, d_model)
    alpha = (2 * depth) ** 0.25
    n_buckets = rel_bias.shape[0]

    bucket = _bucket_table(n_buckets, 2 * MOBA_BLOCK + tdec)
    far_rel = int(np.argmax(bucket == n_buckets - 1))
    assert (bucket[far_rel:] == n_buckets - 1).all() and far_rel <= MOBA_BLOCK
    d_own = np.arange(MOBA_BLOCK)[None, :] - np.arange(MOBA_BLOCK)[:, None]
    bko = jnp.asarray(np.where(d_own >= 0, bucket[np.maximum(d_own, 0)], -1).astype(np.int32))
    bkp = jnp.asarray(bucket[d_own + MOBA_BLOCK].astype(np.int32))
    row_q = np.repeat(np.arange(tdec), N_HEADS)[:, None]
    row_h = np.tile(np.arange(N_HEADS), tdec)[:, None]
    lane_r = np.repeat(np.arange(MOBA_BLOCK), N_HEADS)[None, :]
    lane_h = np.tile(np.arange(N_HEADS), MOBA_BLOCK)[None, :]
    bks = jnp.asarray(np.where(lane_h == row_h, bucket[MOBA_BLOCK + row_q - lane_r], -1).astype(np.int32))
    new_r = np.repeat(np.arange(tdec), N_HEADS)[None, :]
    new_h = np.tile(np.arange(N_HEADS), tdec)[None, :]
    d_new = row_q - new_r
    bkn = jnp.asarray(np.where((new_h == row_h) & (d_new >= 0), bucket[np.maximum(d_new, 0)], -1).astype(np.int32))
    relt = jnp.tile(rel_bias.T.astype(F32), (tdec, 1))

    xp = x_prompt
    xs = x_sample.transpose(1, 0, 2).reshape(tdec * dec_b, d_model)
    outs = {n: [] for n in ("kp", "vp", "ks", "vs", "pp", "ps", "cp", "cs")}
    for l in range(depth):
        win = w_in[l].astype(BF16)
        wpool = _block_diag(w_pool[l]).astype(BF16)
        wl = (win, wpool, s_pool[l][None], w_dw[l], b_dw[l][None], conv_ln_g[l][None], conv_ln_b[l][None],
              w_pw2[l].astype(BF16), p_a[l].astype(BF16), p_c[l].astype(BF16))
        pb, wout, lng, lnb = p_b[l].astype(BF16), w_out[l].astype(BF16), ln_g[l][None], ln_b[l][None]

        k, v, kb, vt, qt, kmean, gbs, smb, hac, pst, cst = _proj_prompt(xp, *wl, sp)
        xp = _attn_prompt(rel_bias.astype(F32), qt, kb, vt, kmean.reshape(bsz, -1, d_attn), bko, bkp, xp, hac, smb,
                          gbs, pb, wout, lng, lnb, alpha)
        outs["kp"].append(k.reshape(bsz, t, N_HEADS, HEAD_DIM))
        outs["vp"].append(v.reshape(bsz, t, N_HEADS, HEAD_DIM))
        outs["pp"].append(pst)
        outs["cp"].append(cst)

        pst_tm = state_pool[:, l].transpose(1, 0, 2).reshape(-1, d_pool)
        cst_tm = state_conv[:, l].transpose(1, 0, 2).reshape(-1, d_conv)
        q_s, k_s, v_s, gbs_s, smb_s, hac_s, pst_s, cst_s = _proj_sample(xs, pst_tm, cst_tm, *wl, sp, dec_b)
        to_b = lambda a: a.reshape(tdec, dec_b, N_HEADS, HEAD_DIM).transpose(1, 0, 2, 3)
        q_b, k_b, v_b = to_b(q_s), to_b(k_s), to_b(v_s)
        yb_s = _attn_sample(page_table, cache_k, cache_v, l, q_b.reshape(dec_b, tdec * N_HEADS, HEAD_DIM), k_b, v_b,
                            relt, bks, bkn)
        yb_tm = yb_s.reshape(dec_b, tdec, d_attn).transpose(1, 0, 2).reshape(tdec * dec_b, d_attn)
        xs = _merge_sample(yb_tm, gbs_s, hac_s, smb_s, xs, pb, wout, lng, lnb, alpha)
        outs["ks"].append(k_b)
        outs["vs"].append(v_b)
        outs["ps"].append(pst_s.reshape(-1, dec_b, d_pool).transpose(1, 0, 2))
        outs["cs"].append(cst_s.reshape(-1, dec_b, d_conv).transpose(1, 0, 2))

    st = lambda n: jnp.stack(outs[n], axis=1)
    y_sample = xs.reshape(tdec, dec_b, d_model).transpose(1, 0, 2)
    return (xp, y_sample, st("kp"), st("vp"), st("ks"), st("vs"), st("pp"), st("ps"), st("cp"), st("cs"))
```

```python
import functools
import math

import numpy as np
import jax
import jax.numpy as jnp
from jax import lax
from jax.experimental import pallas as pl
from jax.experimental.pallas import tpu as pltpu

POOL_WINDOWS = (2, 4, 8, 16)
POOL_HALO = 16
CONV_W = 31
CONV_HALO = 32
N_HEADS = 8
HEAD_DIM = 64
MOBA_BLOCK = 256
MOBA_TOPK = 3
REL_MAX_DIST = 128
LN_EPS = 1e-5
NEG = -1e30
LOG2E = 1.4426950408889634
LANES = 128
VMEM_LIMIT = 56 * 1024 * 1024
SAMPLE_BLOCKS_PER_STEP = 4
FAR_UNROLL = 4

F32 = jnp.float32
BF16 = jnp.bfloat16


def _sigmoid(x):
    return 1.0 / (1.0 + jnp.exp(-x))


def _silu(x):
    return x * _sigmoid(x)


def _layer_norm(x, g, b):
    mu = jnp.mean(x, axis=-1, keepdims=True)
    xc = x - mu
    var = jnp.mean(xc * xc, axis=-1, keepdims=True)
    return xc * lax.rsqrt(var + LN_EPS) * g + b


def _bucket_table(n_buckets, n):
    rel = np.arange(n)
    max_exact = n_buckets // 2
    nf = np.maximum(rel, 1).astype(np.float32)
    large = max_exact + (np.log(nf / np.float32(max_exact)) / np.float32(math.log(REL_MAX_DIST / max_exact))
                         * np.float32(n_buckets - max_exact)).astype(np.int32)
    large = np.minimum(large, n_buckets - 1)
    return np.where(rel < max_exact, rel, large).astype(np.int32)


def _splits(d_pool, d_attn, d_conv, d_model):
    names = ("u_a", "g_a", "q", "k", "v", "g_b", "c_val", "c_gate", "g_c", "m_a", "m_b", "m_c")
    widths = (d_pool, d_pool, d_attn, d_attn, d_attn, d_attn, d_conv, d_conv, d_conv, d_model, d_model, d_model)
    off, out = 0, {}
    for nme, w in zip(names, widths):
        out[nme] = (off, off + w)
        off += w
    return out


def _top_k_mask(g, n_iota, k, n):
    sel = jnp.zeros(g.shape, jnp.bool_)
    for _ in range(k):
        mx = jnp.max(g, axis=0, keepdims=True)
        idx = jnp.min(jnp.where(g == mx, n_iota, n), axis=0, keepdims=True)
        pick = n_iota == idx
        sel = jnp.logical_or(sel, pick)
        g = jnp.where(pick, -jnp.inf, g)
    return sel


def _branches(seg, extu_ref, exta_ref, cnt, wpool_ref, spool_ref, wdw_ref, bdw_ref, clng_ref, clnb_ref,
              wpw2_ref, pa_ref, pc_ref, *, tm, rs, d_pool):
    u = seg("u_a")
    extu_ref[pl.ds(POOL_HALO * rs, tm), :] = u
    gdim = d_pool // len(POOL_WINDOWS)
    lane = lax.broadcasted_iota(jnp.int32, (1, d_pool), 1)
    acc = u
    num = None
    j = 1
    for g, w in enumerate(POOL_WINDOWS):
        while j < w:
            acc = acc + extu_ref[pl.ds((POOL_HALO - j) * rs, tm), :]
            j += 1
        num = acc if num is None else jnp.where(lane >= g * gdim, acc, num)
    pooled = num / cnt
    r = pooled - u
    ya = jnp.dot(r.astype(BF16), wpool_ref[...], preferred_element_type=F32) * spool_ref[...]
    ya = ya * _silu(seg("g_a"))
    hac = _sigmoid(seg("m_a")) * jnp.dot(ya.astype(BF16), pa_ref[...], preferred_element_type=F32)

    a = seg("c_val") * _sigmoid(seg("c_gate"))
    exta_ref[pl.ds(CONV_HALO * rs, tm), :] = a
    h = jnp.zeros_like(a) + bdw_ref[...]
    for j in range(CONV_W):
        h = h + wdw_ref[j:j + 1, :] * exta_ref[pl.ds((CONV_HALO - (CONV_W - 1) + j) * rs, tm), :]
    h = _silu(_layer_norm(h, clng_ref[...], clnb_ref[...]))
    yc = jnp.dot(h.astype(BF16), wpw2_ref[...], preferred_element_type=F32) * _silu(seg("g_c"))
    hac = hac + _sigmoid(seg("m_c")) * jnp.dot(yc.astype(BF16), pc_ref[...], preferred_element_type=F32)
    return hac


def _merge(yb, gbs, hac, smb, x, pb_ref, wout_ref, lng_ref, lnb_ref, alpha):
    ybg = (yb * gbs).astype(BF16)
    h = hac + smb * jnp.dot(ybg, pb_ref[...], preferred_element_type=F32)
    y = alpha * x + jnp.dot(h.astype(BF16), wout_ref[...], preferred_element_type=F32)
    return _layer_norm(y, lng_ref[...], lnb_ref[...])


def _proj_prompt_kernel(x_ref, win_ref, wpool_ref, spool_ref, wdw_ref, bdw_ref, clng_ref, clnb_ref, wpw2_ref,
                        pa_ref, pc_ref, kall_in_ref, vall_in_ref,
                        kall_ref, vall_ref, kb_ref, vt_ref, qt_ref, kmean_ref, gbs_ref, smb_ref, hac_ref, pst_ref,
                        cst_ref, extu_ref, exta_ref, *, tm, sp, d_pool, n_tiles):
    del kall_in_ref, vall_in_ref
    i = pl.program_id(1)
    xb = x_ref[0].astype(BF16)

    def seg(name):
        a, b = sp[name]
        return jnp.dot(xb, win_ref[:, a:b], preferred_element_type=F32)

    @pl.when(i == 0)
    def _():
        extu_ref[0:POOL_HALO, :] = jnp.zeros((POOL_HALO, d_pool), F32)
        exta_ref[0:CONV_HALO, :] = jnp.zeros((CONV_HALO, exta_ref.shape[1]), F32)

    pos1 = i * tm + lax.broadcasted_iota(jnp.int32, (tm, 1), 0) + 1
    wl = jnp.zeros((1, d_pool), jnp.int32)
    lane = lax.broadcasted_iota(jnp.int32, (1, d_pool), 1)
    gdim = d_pool // len(POOL_WINDOWS)
    for g, w in enumerate(POOL_WINDOWS):
        wl = jnp.where(lane >= g * gdim, w, wl)
    cnt = jnp.minimum(pos1, wl).astype(F32)

    hac_ref[0] = _branches(seg, extu_ref, exta_ref, cnt, wpool_ref, spool_ref, wdw_ref, bdw_ref, clng_ref,
                           clnb_ref, wpw2_ref, pa_ref, pc_ref, tm=tm, rs=1, d_pool=d_pool)

    @pl.when(i == n_tiles - 1)
    def _():
        pst_ref[0] = extu_ref[pl.ds(tm + 1, POOL_HALO - 1), :]
        cst_ref[0] = exta_ref[pl.ds(tm + CONV_HALO - (CONV_W - 1), CONV_W - 1), :]

    extu_ref[0:POOL_HALO, :] = extu_ref[pl.ds(tm, POOL_HALO), :]
    exta_ref[0:CONV_HALO, :] = exta_ref[pl.ds(tm, CONV_HALO), :]

    q = seg("q")
    qt_ref[0, 0] = q.T
    k = seg("k")
    kall_ref[0, 0] = k.T
    kb_ref[0, 0] = k.astype(BF16)
    kmean_ref[0, 0] = jnp.sum(k, axis=0, keepdims=True) * (1.0 / tm)
    vt = seg("v").T
    vall_ref[0, 0] = vt
    vt_ref[0, 0] = vt.astype(BF16)
    gbs_ref[0] = _silu(seg("g_b"))
    smb_ref[0] = _sigmoid(seg("m_b"))


def _const_spec(shape):
    nd = len(shape)
    return pl.BlockSpec(shape, lambda *_: (0,) * nd, pipeline_mode=pl.Buffered(1))


def _proj_prompt(x, win, wpool, spool, wdw, bdw, clng, clnb, wpw2, pa, pc, kall, vall, layer, sp):
    bsz, t, d_model = x.shape
    tm = MOBA_BLOCK
    n_tiles = t // tm
    d_pool, d_conv, d_attn = wpool.shape[0], wpw2.shape[0], sp["q"][1] - sp["q"][0]
    kern = functools.partial(_proj_prompt_kernel, tm=tm, sp=sp, d_pool=d_pool, n_tiles=n_tiles)
    row = lambda w: pl.BlockSpec((1, tm, w), lambda b, i: (b, i, 0))
    blk = lambda r, c: pl.BlockSpec((1, 1, r, c), lambda b, i: (b, i, 0, 0))
    out_shape = (
        jax.ShapeDtypeStruct(kall.shape, F32),
        jax.ShapeDtypeStruct(vall.shape, F32),
        jax.ShapeDtypeStruct((bsz, n_tiles, tm, d_attn), BF16),
        jax.ShapeDtypeStruct((bsz, n_tiles, d_attn, tm), BF16),
        jax.ShapeDtypeStruct((bsz, n_tiles, d_attn, tm), F32),
        jax.ShapeDtypeStruct((bsz, n_tiles, 1, d_attn), F32),
        jax.ShapeDtypeStruct((bsz, t, d_attn), F32),
        jax.ShapeDtypeStruct((bsz, t, d_model), F32),
        jax.ShapeDtypeStruct((bsz, t, d_model), F32),
        jax.ShapeDtypeStruct((bsz, POOL_HALO - 1, d_pool), F32),
        jax.ShapeDtypeStruct((bsz, CONV_W - 1, d_conv), F32),
    )
    tall = pl.BlockSpec((1, 1, d_attn, tm), lambda b, i: (b, layer, 0, i))
    out_specs = (
        tall, tall, blk(tm, d_attn), blk(d_attn, tm), blk(d_attn, tm), blk(1, d_attn),
        row(d_attn), row(d_model), row(d_model),
        pl.BlockSpec((1, POOL_HALO - 1, d_pool), lambda b, i: (b, 0, 0)),
        pl.BlockSpec((1, CONV_W - 1, d_conv), lambda b, i: (b, 0, 0)),
    )
    in_specs = ([row(d_model)] + [_const_spec(a.shape) for a in (win, wpool, spool, wdw, bdw, clng, clnb, wpw2, pa, pc)]
                + [pl.BlockSpec(memory_space=pl.ANY)] * 2)
    return pl.pallas_call(
        kern, grid=(bsz, n_tiles), in_specs=in_specs, out_specs=out_specs, out_shape=out_shape,
        input_output_aliases={11: 0, 12: 1},
        scratch_shapes=[pltpu.VMEM((POOL_HALO + tm, d_pool), F32), pltpu.VMEM((CONV_HALO + tm, d_conv), F32)],
        compiler_params=pltpu.CompilerParams(dimension_semantics=("arbitrary", "arbitrary"),
                                             vmem_limit_bytes=VMEM_LIMIT),
        name="proj_prompt",
    )(x, win, wpool, spool, wdw, bdw, clng, clnb, wpw2, pa, pc, kall, vall)


def _attn_prompt_kernel(rel_ref, qt_ref, kb_ref, vt_ref, kmean_ref, bko_ref, bkp_ref, x_ref, hac_ref, smb_ref,
                        gbs_ref, pb_ref, wout_ref, lng_ref, lnb_ref, o_ref,
                        bown_ref, bprev_ref, msk_ref, qb_ref, s_ref, p_ref, m_ref, l_ref, a_ref, acc_ref, yb_ref,
                        *, nb, tq, alpha, n_buckets):
    i = pl.program_id(1)

    @pl.when(jnp.logical_and(pl.program_id(0) == 0, i == 0))
    def _():
        for h in range(N_HEADS):
            bown_ref[h] = jnp.full((tq, tq), NEG, F32)
            bprev_ref[h] = jnp.zeros((tq, tq), F32)

            def fill(bkt, c):
                val = rel_ref[bkt, h] * LOG2E
                bown_ref[h] = jnp.where(bko_ref[...] == bkt, val, bown_ref[h])
                bprev_ref[h] = jnp.where(bkp_ref[...] == bkt, val, bprev_ref[h])
                return c

            lax.fori_loop(0, n_buckets, fill, 0)

    n_iota = lax.broadcasted_iota(jnp.int32, (nb, tq), 0)
    past = n_iota < i
    half = lax.broadcasted_iota(jnp.int32, (2 * HEAD_DIM, tq), 0) // HEAD_DIM
    pair = lambda h: slice((h // 2) * 2 * HEAD_DIM, (h // 2 + 1) * 2 * HEAD_DIM)
    rows = lambda h: slice(h * HEAD_DIM, (h + 1) * HEAD_DIM)

    for h in range(N_HEADS):
        qh = jnp.where(half == h % 2, qt_ref[0, 0, pair(h), :], 0.0)
        gate = jnp.dot(kmean_ref[0, :, pair(h)], qh, preferred_element_type=F32,
                       precision=lax.Precision.HIGHEST)
        sel = _top_k_mask(jnp.where(past, gate, NEG), n_iota, min(MOBA_TOPK, nb), nb)
        msk_ref[h] = jnp.where(jnp.logical_and(sel, past), 0.0, NEG)
        qb_ref[h] = (qh * (HEAD_DIM ** -0.5 * LOG2E)).astype(BF16)
        m_ref[h] = jnp.full((1, tq), NEG, F32)
        l_ref[h] = jnp.zeros((1, tq), F32)
        acc_ref[h] = jnp.zeros((HEAD_DIM, tq), F32)

    def scores(n, slot):
        for h in range(N_HEADS):
            s_ref[slot, h] = jnp.dot(kb_ref[0, n, :, pair(h)], qb_ref[h], preferred_element_type=F32)

    def softmax_pv(n, slot, bias_ref, shift_of):
        for h in range(N_HEADS):
            s = s_ref[slot, h] if bias_ref is None else s_ref[slot, h] + bias_ref[h]
            shift = shift_of(h)
            m_blk = jnp.max(s, axis=0, keepdims=True)
            m_old = m_ref[h]
            if shift is None:
                m_new = jnp.maximum(m_old, m_blk)
                p = jnp.exp2(s - m_new)
            else:
                m_new = jnp.maximum(m_old, m_blk + shift)
                p = jnp.exp2(s + (shift - m_new))
            a = jnp.exp2(m_old - m_new)
            l_ref[h] = a * l_ref[h] + jnp.sum(p, axis=0, keepdims=True)
            a_ref[h] = a
            m_ref[h] = m_new
            p_ref[h] = p.astype(BF16)
        for h in range(N_HEADS):
            acc_ref[h] = a_ref[h] * acc_ref[h] + jnp.dot(vt_ref[0, n, rows(h), :], p_ref[h],
                                                         preferred_element_type=F32)

    def far_shift(n):
        return lambda h: msk_ref[h, pl.ds(n, 1), :] + rel_ref[n_buckets - 1, h] * LOG2E

    def far_group(k, c):
        for u in range(FAR_UNROLL):
            scores(FAR_UNROLL * k + u, u)
        for u in range(FAR_UNROLL):
            softmax_pv(FAR_UNROLL * k + u, u, None, far_shift(FAR_UNROLL * k + u))
        return c

    def far_single(n, c):
        scores(n, 0)
        softmax_pv(n, 0, None, far_shift(n))
        return c

    n_far = jnp.maximum(i - 1, 0)
    n_grp = n_far // FAR_UNROLL
    lax.fori_loop(0, n_grp, far_group, 0)
    lax.fori_loop(n_grp * FAR_UNROLL, n_far, far_single, 0)

    @pl.when(i >= 1)
    def _():
        scores(i - 1, 0)
        softmax_pv(i - 1, 0, bprev_ref, lambda h: msk_ref[h, pl.ds(i - 1, 1), :])

    scores(i, 1)
    softmax_pv(i, 1, bown_ref, lambda h: None)

    for h in range(N_HEADS):
        o = acc_ref[h] / l_ref[h]
        yb_ref[:, rows(h)] = o.T

    o_ref[0] = _merge(yb_ref[...], gbs_ref[0], hac_ref[0], smb_ref[0], x_ref[0], pb_ref, wout_ref, lng_ref,
                      lnb_ref, alpha)


def _attn_prompt(rel_bias, qt, kb, vt, kmean, bko, bkp, x, hac, smb, gbs, pb, wout, lng, lnb, alpha):
    bsz, t, d_model = x.shape
    nb, tq = kb.shape[1], kb.shape[2]
    d_attn = kb.shape[3]
    kern = functools.partial(_attn_prompt_kernel, nb=nb, tq=tq, alpha=alpha, n_buckets=rel_bias.shape[0])
    row = lambda w: pl.BlockSpec((1, tq, w), lambda b, i: (b, i, 0))
    per_b = lambda a: pl.BlockSpec((1,) + a.shape[1:], lambda b, i: (b,) + (0,) * (a.ndim - 1),
                                   pipeline_mode=pl.Buffered(1))
    in_specs = [
        pl.BlockSpec(memory_space=pltpu.SMEM),
        pl.BlockSpec((1, 1, d_attn, tq), lambda b, i: (b, i, 0, 0)),
        per_b(kb), per_b(vt), per_b(kmean),
        _const_spec(bko.shape), _const_spec(bkp.shape),
        row(d_model), row(d_model), row(d_model), row(d_attn),
        _const_spec(pb.shape), _const_spec(wout.shape), _const_spec(lng.shape), _const_spec(lnb.shape),
    ]
    return pl.pallas_call(
        kern, grid=(bsz, nb), in_specs=in_specs, out_specs=row(d_model),
        out_shape=jax.ShapeDtypeStruct((bsz, t, d_model), F32),
        scratch_shapes=[pltpu.VMEM((N_HEADS, tq, tq), F32), pltpu.VMEM((N_HEADS, tq, tq), F32),
                        pltpu.VMEM((N_HEADS, nb, tq), F32), pltpu.VMEM((N_HEADS, 2 * HEAD_DIM, tq), BF16),
                        pltpu.VMEM((FAR_UNROLL, N_HEADS, tq, tq), F32), pltpu.VMEM((N_HEADS, tq, tq), BF16),
                        pltpu.VMEM((N_HEADS, 1, tq), F32), pltpu.VMEM((N_HEADS, 1, tq), F32),
                        pltpu.VMEM((N_HEADS, 1, tq), F32),
                        pltpu.VMEM((N_HEADS, HEAD_DIM, tq), F32), pltpu.VMEM((tq, d_attn), F32)],
        compiler_params=pltpu.CompilerParams(dimension_semantics=("arbitrary", "arbitrary"),
                                             vmem_limit_bytes=VMEM_LIMIT),
        name="attn_prompt",
    )(rel_bias, qt, kb, vt, kmean, bko, bkp, x, hac, smb, gbs, pb, wout, lng, lnb)


def _proj_sample_kernel(x_ref, pst_in_ref, cst_in_ref, win_ref, wpool_ref, spool_ref, wdw_ref, bdw_ref, clng_ref,
                        clnb_ref, wpw2_ref, pa_ref, pc_ref,
                        q_ref, k_ref, v_ref, gbs_ref, smb_ref, hac_ref, pst_ref, cst_ref,
                        extu_ref, exta_ref, *, tm, rs, sp, d_pool):
    xb = x_ref[...].astype(BF16)

    def seg(name):
        a, b = sp[name]
        return jnp.dot(xb, win_ref[:, a:b], preferred_element_type=F32)

    n_pool, n_conv = (POOL_HALO - 1) * rs, (CONV_W - 1) * rs
    extu_ref[0:rs, :] = jnp.zeros((rs, d_pool), F32)
    extu_ref[pl.ds(rs, n_pool), :] = pst_in_ref[...]
    exta_ref[0:(CONV_HALO - CONV_W + 1) * rs, :] = jnp.zeros(((CONV_HALO - CONV_W + 1) * rs, exta_ref.shape[1]), F32)
    exta_ref[pl.ds((CONV_HALO - CONV_W + 1) * rs, n_conv), :] = cst_in_ref[...]

    wl = jnp.zeros((1, d_pool), jnp.int32)
    lane = lax.broadcasted_iota(jnp.int32, (1, d_pool), 1)
    gdim = d_pool // len(POOL_WINDOWS)
    for g, w in enumerate(POOL_WINDOWS):
        wl = jnp.where(lane >= g * gdim, w, wl)
    cnt = wl.astype(F32)

    hac_ref[...] = _branches(seg, extu_ref, exta_ref, cnt, wpool_ref, spool_ref, wdw_ref, bdw_ref, clng_ref,
                             clnb_ref, wpw2_ref, pa_ref, pc_ref, tm=tm, rs=rs, d_pool=d_pool)
    pst_ref[...] = extu_ref[pl.ds(POOL_HALO * rs + tm - n_pool, n_pool), :]
    cst_ref[...] = exta_ref[pl.ds(CONV_HALO * rs + tm - n_conv, n_conv), :]
    q_ref[...] = seg("q")
    k_ref[...] = seg("k")
    v_ref[...] = seg("v")
    gbs_ref[...] = _silu(seg("g_b"))
    smb_ref[...] = _sigmoid(seg("m_b"))


def _proj_sample(x_tm, pst_tm, cst_tm, win, wpool, spool, wdw, bdw, clng, clnb, wpw2, pa, pc, sp, rs):
    tm, d_model = x_tm.shape
    d_pool, d_conv, d_attn = wpool.shape[0], wpw2.shape[0], sp["q"][1] - sp["q"][0]
    kern = functools.partial(_proj_sample_kernel, tm=tm, rs=rs, sp=sp, d_pool=d_pool)
    out_shape = (
        jax.ShapeDtypeStruct((tm, d_attn), F32), jax.ShapeDtypeStruct((tm, d_attn), F32),
        jax.ShapeDtypeStruct((tm, d_attn), F32), jax.ShapeDtypeStruct((tm, d_attn), F32),
        jax.ShapeDtypeStruct((tm, d_model), F32), jax.ShapeDtypeStruct((tm, d_model), F32),
        jax.ShapeDtypeStruct(pst_tm.shape, F32), jax.ShapeDtypeStruct(cst_tm.shape, F32),
    )
    return pl.pallas_call(
        kern, out_shape=out_shape,
        scratch_shapes=[pltpu.VMEM((POOL_HALO * rs + tm, d_pool), F32), pltpu.VMEM((CONV_HALO * rs + tm, d_conv), F32)],
        compiler_params=pltpu.CompilerParams(vmem_limit_bytes=VMEM_LIMIT),
        name="proj_sample",
    )(x_tm, pst_tm, cst_tm, win, wpool, spool, wdw, bdw, clng, clnb, wpw2, pa, pc)


def _attn_sample_kernel(pt_ref, rel_ref, *refs, nb, bps, tdec, n_buckets):
    npg = 2 * bps
    k_refs, v_refs = refs[:npg], refs[npg:2 * npg]
    (q_ref, knew_ref, vnew_ref, bks_ref, bkn_ref, o_ref,
     blast_ref, cfar_ref, bown_ref, ms_ref, ls_ref, gs_ref, r_ref) = refs[2 * npg:]
    jj = pl.program_id(1)
    scale = HEAD_DIM ** -0.5
    bnn = (((2,), (1,)), ((0,), (0,)))
    bnt = (((2,), (2,)), ((0,), (0,)))

    @pl.when(jnp.logical_and(pl.program_id(0) == 0, jj == 0))
    def _():
        for h in range(N_HEADS):
            blast_ref[h] = jnp.zeros(blast_ref.shape[1:], F32)
            bown_ref[h] = jnp.full(bown_ref.shape[1:], NEG, F32)
            cfar_ref[h] = jnp.full(cfar_ref.shape[1:], rel_ref[n_buckets - 1, h], F32)

            def fill(bkt, c):
                val = rel_ref[bkt, h]
                blast_ref[h] = jnp.where(bks_ref[...] == bkt, val, blast_ref[h])
                bown_ref[h] = jnp.where(bkn_ref[...] == bkt, val, bown_ref[h])
                return c

            lax.fori_loop(0, n_buckets, fill, 0)

    q3 = q_ref[0]
    qs = (q3 * scale).astype(BF16)

    raw = [jnp.concatenate(
        [lax.dot_general(qs, k_refs[2 * c + o][0, 0].astype(BF16), bnn, preferred_element_type=F32)
         for o in range(2)], axis=-1) for c in range(bps)]
    probs = []
    for c in range(bps):
        j = jj * bps + c
        gs_ref[j] = jnp.sum(raw[c], axis=-1, keepdims=True) * (1.0 / (MOBA_BLOCK * scale))
        if c == bps - 1:
            s = raw[c] + jnp.where(j == nb - 1, blast_ref[...], cfar_ref[...])
        else:
            s = raw[c] + cfar_ref[...]
        m = jnp.max(s, axis=-1, keepdims=True)
        p = jnp.exp(s - m)
        ms_ref[j] = m
        ls_ref[j] = jnp.sum(p, axis=-1, keepdims=True)
        probs.append(p.astype(BF16))
    for c in range(bps):
        pb = probs[c]
        half = pb.shape[-1] // 2
        r_ref[jj * bps + c] = (
            lax.dot_general(pb[..., :half], v_refs[2 * c][0, 0].astype(BF16), bnt, preferred_element_type=F32)
            + lax.dot_general(pb[..., half:], v_refs[2 * c + 1][0, 0].astype(BF16), bnt,
                              preferred_element_type=F32))

    @pl.when(jj == nb // bps - 1)
    def _():
        knew, vnew = knew_ref[0], vnew_ref[0]
        qf = q3 * scale
        s_o = jnp.concatenate([jnp.sum(qf * knew[:, t:t + 1, :], axis=-1, keepdims=True) for t in range(tdec)],
                              axis=-1) + bown_ref[...]
        m_o = jnp.max(s_o, axis=-1, keepdims=True)
        p_o = jnp.exp(s_o - m_o)
        l_o = jnp.sum(p_o, axis=-1, keepdims=True)
        r_o = p_o[..., 0:1] * vnew[:, 0:1, :]
        for t in range(1, tdec):
            r_o = r_o + p_o[..., t:t + 1] * vnew[:, t:t + 1, :]
        g = gs_ref[...]
        n_iota = lax.broadcasted_iota(jnp.int32, g.shape, 0)
        sel = _top_k_mask(g, n_iota, min(MOBA_TOPK, nb), nb)
        ms = jnp.where(sel, ms_ref[...], NEG)
        mtot = jnp.maximum(jnp.max(ms, axis=0), m_o)
        w = jnp.where(sel, jnp.exp(ms - mtot[None]), 0.0)
        w_o = jnp.exp(m_o - mtot)
        ltot = w_o * l_o + jnp.sum(w * ls_ref[...], axis=0)
        acc = w_o * r_o + jnp.sum(w * r_ref[...], axis=0)
        o_ref[0] = acc / ltot


def _attn_sample(page_table, rel_bias, cache_kt, cache_vt, layer, q3, knew, vnew, bks, bkn):
    dec_b, _, qp, _ = q3.shape
    tdec = knew.shape[2]
    page = cache_kt.shape[4]
    assert 2 * page == MOBA_BLOCK, "one MoBA block is swept as two cache pages"
    nb = page_table.shape[1] // 2
    bps = math.gcd(SAMPLE_BLOCKS_PER_STEP, nb)
    kern = functools.partial(_attn_sample_kernel, nb=nb, bps=bps, tdec=tdec, n_buckets=rel_bias.shape[0])
    pg = lambda o: pl.BlockSpec((1, 1, N_HEADS, HEAD_DIM, page),
                                lambda b, jj, pt: (pt[b, 2 * bps * jj + o], layer, 0, 0, 0))
    full = lambda a: pl.BlockSpec(a.shape, lambda b, jj, pt: (0,) * a.ndim)
    per_b = lambda a: pl.BlockSpec((1,) + a.shape[1:], lambda b, jj, pt: (b,) + (0,) * (a.ndim - 1))
    stat = pltpu.VMEM((nb, N_HEADS, qp, 1), F32)
    grid_spec = pltpu.PrefetchScalarGridSpec(
        num_scalar_prefetch=1, grid=(dec_b, nb // bps),
        in_specs=([pl.BlockSpec(memory_space=pltpu.SMEM)] + [pg(o) for o in range(2 * bps)] * 2
                  + [per_b(q3), per_b(knew), per_b(vnew), full(bks), full(bkn)]),
        out_specs=pl.BlockSpec((1, N_HEADS, qp, HEAD_DIM), lambda b, jj, pt: (b, 0, 0, 0)),
        scratch_shapes=[pltpu.VMEM((N_HEADS,) + bks.shape, F32), pltpu.VMEM((N_HEADS,) + bks.shape, F32),
                        pltpu.VMEM((N_HEADS,) + bkn.shape, F32), stat, stat, stat,
                        pltpu.VMEM((nb, N_HEADS, qp, HEAD_DIM), F32)],
    )
    return pl.pallas_call(
        kern, grid_spec=grid_spec, out_shape=jax.ShapeDtypeStruct((dec_b, N_HEADS, qp, HEAD_DIM), F32),
        compiler_params=pltpu.CompilerParams(dimension_semantics=("arbitrary", "arbitrary"),
                                             vmem_limit_bytes=VMEM_LIMIT),
        name="attn_sample",
    )(page_table, rel_bias, *([cache_kt] * (2 * bps)), *([cache_vt] * (2 * bps)), q3, knew, vnew, bks, bkn)


def _merge_sample_kernel(yb_ref, gbs_ref, hac_ref, smb_ref, x_ref, pb_ref, wout_ref, lng_ref, lnb_ref, o_ref, *, alpha):
    o_ref[...] = _merge(yb_ref[...], gbs_ref[...], hac_ref[...], smb_ref[...], x_ref[...], pb_ref, wout_ref,
                        lng_ref, lnb_ref, alpha)


def _merge_sample(yb, gbs, hac, smb, x, pb, wout, lng, lnb, alpha):
    return pl.pallas_call(
        functools.partial(_merge_sample_kernel, alpha=alpha),
        out_shape=jax.ShapeDtypeStruct(x.shape, F32),
        compiler_params=pltpu.CompilerParams(vmem_limit_bytes=VMEM_LIMIT),
        name="merge_sample",
    )(yb, gbs, hac, smb, x, pb, wout, lng, lnb)


def _block_diag(w):
    g, c, d = w.shape
    out = jnp.zeros((g * c, g * d), w.dtype)
    for i in range(g):
        out = out.at[i * c:(i + 1) * c, i * d:(i + 1) * d].set(w[i])
    return out


def kernel(x_prompt, x_sample, cache_k, cache_v, page_table, state_pool, state_conv, rel_bias, w_in, w_pool, s_pool,
           w_dw, b_dw, conv_ln_g, conv_ln_b, w_pw2, p_a, p_b, p_c, w_out, ln_g, ln_b):
    depth = w_in.shape[0]
    bsz, t, d_model = x_prompt.shape
    dec_b, tdec, _ = x_sample.shape
    d_pool = w_pool.shape[1] * w_pool.shape[2]
    d_attn, d_conv = p_b.shape[1], p_c.shape[1]
    assert d_attn == N_HEADS * HEAD_DIM and t % MOBA_BLOCK == 0
    page = cache_k.shape[2]
    past_len = page_table.shape[1] * page
    assert past_len % MOBA_BLOCK == 0 and past_len >= POOL_HALO
    sp = _splits(d_pool, d_attn, d_conv, d_model)
    alpha = (2 * depth) ** 0.25
    n_buckets = rel_bias.shape[0]

    bucket = _bucket_table(n_buckets, 2 * MOBA_BLOCK + tdec)
    far_rel = int(np.argmax(bucket == n_buckets - 1))
    assert (bucket[far_rel:] == n_buckets - 1).all() and far_rel <= MOBA_BLOCK
    d_own = np.arange(MOBA_BLOCK)[None, :] - np.arange(MOBA_BLOCK)[:, None]
    bko = jnp.asarray(np.where(d_own >= 0, bucket[np.maximum(d_own, 0)], -1).astype(np.int32))
    bkp = jnp.asarray(bucket[d_own + MOBA_BLOCK].astype(np.int32))
    qp = -(-tdec // 8) * 8
    dq = np.minimum(np.arange(qp), tdec - 1)[:, None]
    bks = jnp.asarray(bucket[MOBA_BLOCK + dq - np.arange(MOBA_BLOCK)[None, :]].astype(np.int32))
    d_new = dq - np.arange(tdec)[None, :]
    bkn = jnp.asarray(np.where(d_new >= 0, bucket[np.maximum(d_new, 0)], -1).astype(np.int32))
    rel = rel_bias.astype(F32)
    cache_kt = cache_k.transpose(0, 1, 3, 4, 2)
    cache_vt = cache_v.transpose(0, 1, 3, 4, 2)
    kall = jnp.zeros((bsz, depth, d_attn, t), F32)
    vall = jnp.zeros((bsz, depth, d_attn, t), F32)

    xp = x_prompt
    xs = x_sample.transpose(1, 0, 2).reshape(tdec * dec_b, d_model)
    outs = {n: [] for n in ("ks", "vs", "pp", "ps", "cp", "cs")}
    for l in range(depth):
        win = w_in[l].astype(BF16)
        wpool = _block_diag(w_pool[l]).astype(BF16)
        wl = (win, wpool, s_pool[l][None], w_dw[l], b_dw[l][None], conv_ln_g[l][None], conv_ln_b[l][None],
              w_pw2[l].astype(BF16), p_a[l].astype(BF16), p_c[l].astype(BF16))
        pb, wout, lng, lnb = p_b[l].astype(BF16), w_out[l].astype(BF16), ln_g[l][None], ln_b[l][None]

        kall, vall, kb, vt, qt, kmean, gbs, smb, hac, pst, cst = _proj_prompt(xp, *wl, kall, vall, l, sp)
        xp = _attn_prompt(rel, qt, kb, vt, kmean.reshape(bsz, -1, d_attn), bko, bkp, xp, hac, smb, gbs, pb, wout,
                          lng, lnb, alpha)
        outs["pp"].append(pst)
        outs["cp"].append(cst)

        pst_tm = state_pool[:, l].transpose(1, 0, 2).reshape(-1, d_pool)
        cst_tm = state_conv[:, l].transpose(1, 0, 2).reshape(-1, d_conv)
        q_s, k_s, v_s, gbs_s, smb_s, hac_s, pst_s, cst_s = _proj_sample(xs, pst_tm, cst_tm, *wl, sp, dec_b)
        to_h = lambda a: a.reshape(tdec, dec_b, N_HEADS, HEAD_DIM).transpose(1, 2, 0, 3)
        q_h, k_h, v_h = to_h(q_s), to_h(k_s), to_h(v_s)
        q3 = jnp.pad(q_h, ((0, 0), (0, 0), (0, qp - tdec), (0, 0)))
        yb_s = _attn_sample(page_table, rel, cache_kt, cache_vt, l, q3, k_h, v_h, bks, bkn)
        yb_tm = yb_s[:, :, :tdec].transpose(2, 0, 1, 3).reshape(tdec * dec_b, d_attn)
        k_b, v_b = k_h.transpose(0, 2, 1, 3), v_h.transpose(0, 2, 1, 3)
        xs = _merge_sample(yb_tm, gbs_s, hac_s, smb_s, xs, pb, wout, lng, lnb, alpha)
        outs["ks"].append(k_b)
        outs["vs"].append(v_b)
        outs["ps"].append(pst_s.reshape(-1, dec_b, d_pool).transpose(1, 0, 2))
        outs["cs"].append(cst_s.reshape(-1, dec_b, d_conv).transpose(1, 0, 2))

    st = lambda n: jnp.stack(outs[n], axis=1)
    y_sample = xs.reshape(tdec, dec_b, d_model).transpose(1, 0, 2)
    to_out = lambda a: a.reshape(bsz, depth, N_HEADS, HEAD_DIM, t).transpose(0, 1, 4, 2, 3)
    return (xp, y_sample, to_out(kall), to_out(vall), st("ks"), st("vs"), st("pp"), st("ps"), st("cp"), st("cs"))
```

```python
import functools
import math

import numpy as np
import jax
import jax.numpy as jnp
from jax import lax
from jax.experimental import pallas as pl
from jax.experimental.pallas import tpu as pltpu

POOL_WINDOWS = (2, 4, 8, 16)
POOL_HALO = 16
CONV_W = 31
CONV_HALO = 32
N_HEADS = 8
HEAD_DIM = 64
MOBA_BLOCK = 256
MOBA_TOPK = 3
REL_MAX_DIST = 128
LN_EPS = 1e-5
NEG = -1e30
LOG2E = 1.4426950408889634
LANES = 128
VMEM_LIMIT = 56 * 1024 * 1024
SAMPLE_BLOCKS_PER_STEP = 8
FAR_UNROLL = 4

F32 = jnp.float32
BF16 = jnp.bfloat16


def _sigmoid(x):
    return 1.0 / (1.0 + jnp.exp(-x))


def _silu(x):
    return x * _sigmoid(x)


def _layer_norm(x, g, b):
    mu = jnp.mean(x, axis=-1, keepdims=True)
    xc = x - mu
    var = jnp.mean(xc * xc, axis=-1, keepdims=True)
    return xc * lax.rsqrt(var + LN_EPS) * g + b


def _bucket_table(n_buckets, n):
    rel = np.arange(n)
    max_exact = n_buckets // 2
    nf = np.maximum(rel, 1).astype(np.float32)
    large = max_exact + (np.log(nf / np.float32(max_exact)) / np.float32(math.log(REL_MAX_DIST / max_exact))
                         * np.float32(n_buckets - max_exact)).astype(np.int32)
    large = np.minimum(large, n_buckets - 1)
    return np.where(rel < max_exact, rel, large).astype(np.int32)


def _splits(d_pool, d_attn, d_conv, d_model):
    names = ("u_a", "g_a", "q", "k", "v", "g_b", "c_val", "c_gate", "g_c", "m_a", "m_b", "m_c")
    widths = (d_pool, d_pool, d_attn, d_attn, d_attn, d_attn, d_conv, d_conv, d_conv, d_model, d_model, d_model)
    off, out = 0, {}
    for nme, w in zip(names, widths):
        out[nme] = (off, off + w)
        off += w
    return out


def _top_k_mask(g, n_iota, k, n):
    sel = jnp.zeros(g.shape, jnp.bool_)
    for _ in range(k):
        mx = jnp.max(g, axis=0, keepdims=True)
        idx = jnp.min(jnp.where(g == mx, n_iota, n), axis=0, keepdims=True)
        pick = n_iota == idx
        sel = jnp.logical_or(sel, pick)
        g = jnp.where(pick, -jnp.inf, g)
    return sel


def _branches(seg, extu_ref, exta_ref, cnt, wpool_ref, spool_ref, wdw_ref, bdw_ref, clng_ref, clnb_ref,
              wpw2_ref, pa_ref, pc_ref, *, tm, rs, d_pool, between=None):
    extu_ref[pl.ds(POOL_HALO * rs, tm), :] = seg("u_a")
    exta_ref[pl.ds(CONV_HALO * rs, tm), :] = seg("c_val") * _sigmoid(seg("c_gate"))
    if between is not None:
        between[0]()

    u = extu_ref[pl.ds(POOL_HALO * rs, tm), :]
    gdim = d_pool // len(POOL_WINDOWS)
    lane = lax.broadcasted_iota(jnp.int32, (1, d_pool), 1)
    acc = u
    num = None
    j = 1
    for g, w in enumerate(POOL_WINDOWS):
        while j < w:
            acc = acc + extu_ref[pl.ds((POOL_HALO - j) * rs, tm), :]
            j += 1
        num = acc if num is None else jnp.where(lane >= g * gdim, acc, num)
    pooled = num / cnt
    r = pooled - u
    ya = jnp.dot(r.astype(BF16), wpool_ref[...], preferred_element_type=F32) * spool_ref[...]
    ya = ya * _silu(seg("g_a"))
    hac = _sigmoid(seg("m_a")) * jnp.dot(ya.astype(BF16), pa_ref[...], preferred_element_type=F32)
    if between is not None:
        between[1]()

    h = jnp.zeros((tm, exta_ref.shape[1]), F32) + bdw_ref[...]
    for j in range(CONV_W):
        h = h + wdw_ref[j:j + 1, :] * exta_ref[pl.ds((CONV_HALO - (CONV_W - 1) + j) * rs, tm), :]
    h = _silu(_layer_norm(h, clng_ref[...], clnb_ref[...]))
    yc = jnp.dot(h.astype(BF16), wpw2_ref[...], preferred_element_type=F32) * _silu(seg("g_c"))
    hac = hac + _sigmoid(seg("m_c")) * jnp.dot(yc.astype(BF16), pc_ref[...], preferred_element_type=F32)
    return hac


def _merge(yb, gbs, hac, smb, x, pb_ref, wout_ref, lng_ref, lnb_ref, alpha):
    ybg = (yb * gbs).astype(BF16)
    h = hac + smb * jnp.dot(ybg, pb_ref[...], preferred_element_type=F32)
    y = alpha * x + jnp.dot(h.astype(BF16), wout_ref[...], preferred_element_type=F32)
    return _layer_norm(y, lng_ref[...], lnb_ref[...])


def _proj_prompt_kernel(x_ref, win_ref, wpool_ref, spool_ref, wdw_ref, bdw_ref, clng_ref, clnb_ref, wpw2_ref,
                        pa_ref, pc_ref, kall_in_ref, vall_in_ref,
                        kall_ref, vall_ref, kb_ref, vt_ref, qt_ref, kmean_ref, gbs_ref, smb_ref, hac_ref, pst_ref,
                        cst_ref, extu_ref, exta_ref, *, tm, sp, d_pool, n_tiles):
    del kall_in_ref, vall_in_ref
    i = pl.program_id(1)
    xb = x_ref[0].astype(BF16)

    def seg(name):
        a, b = sp[name]
        return jnp.dot(xb, win_ref[:, a:b], preferred_element_type=F32)

    @pl.when(i == 0)
    def _():
        extu_ref[0:POOL_HALO, :] = jnp.zeros((POOL_HALO, d_pool), F32)
        exta_ref[0:CONV_HALO, :] = jnp.zeros((CONV_HALO, exta_ref.shape[1]), F32)

    pos1 = i * tm + lax.broadcasted_iota(jnp.int32, (tm, 1), 0) + 1
    wl = jnp.zeros((1, d_pool), jnp.int32)
    lane = lax.broadcasted_iota(jnp.int32, (1, d_pool), 1)
    gdim = d_pool // len(POOL_WINDOWS)
    for g, w in enumerate(POOL_WINDOWS):
        wl = jnp.where(lane >= g * gdim, w, wl)
    cnt = jnp.minimum(pos1, wl).astype(F32)

    def attention_inputs():
        q = seg("q")
        qt_ref[0, 0] = q.T
        k = seg("k")
        kall_ref[0, 0] = k.T
        kb_ref[0, 0] = k.astype(BF16)
        kmean_ref[0, 0] = jnp.sum(k, axis=0, keepdims=True) * (1.0 / tm)
        vt = seg("v").T
        vall_ref[0, 0] = vt
        vt_ref[0, 0] = vt.astype(BF16)

    def attention_gates():
        gbs_ref[0] = _silu(seg("g_b"))
        smb_ref[0] = _sigmoid(seg("m_b"))

    hac_ref[0] = _branches(seg, extu_ref, exta_ref, cnt, wpool_ref, spool_ref, wdw_ref, bdw_ref, clng_ref,
                           clnb_ref, wpw2_ref, pa_ref, pc_ref, tm=tm, rs=1, d_pool=d_pool,
                           between=(attention_inputs, attention_gates))

    @pl.when(i == n_tiles - 1)
    def _():
        pst_ref[0] = extu_ref[pl.ds(tm + 1, POOL_HALO - 1), :]
        cst_ref[0] = exta_ref[pl.ds(tm + CONV_HALO - (CONV_W - 1), CONV_W - 1), :]

    extu_ref[0:POOL_HALO, :] = extu_ref[pl.ds(tm, POOL_HALO), :]
    exta_ref[0:CONV_HALO, :] = exta_ref[pl.ds(tm, CONV_HALO), :]


def _const_spec(shape):
    nd = len(shape)
    return pl.BlockSpec(shape, lambda *_: (0,) * nd, pipeline_mode=pl.Buffered(1))


def _proj_prompt(x, win, wpool, spool, wdw, bdw, clng, clnb, wpw2, pa, pc, kall, vall, layer, sp):
    bsz, t, d_model = x.shape
    tm = MOBA_BLOCK
    n_tiles = t // tm
    d_pool, d_conv, d_attn = wpool.shape[0], wpw2.shape[0], sp["q"][1] - sp["q"][0]
    kern = functools.partial(_proj_prompt_kernel, tm=tm, sp=sp, d_pool=d_pool, n_tiles=n_tiles)
    row = lambda w: pl.BlockSpec((1, tm, w), lambda b, i: (b, i, 0))
    blk = lambda r, c: pl.BlockSpec((1, 1, r, c), lambda b, i: (b, i, 0, 0))
    out_shape = (
        jax.ShapeDtypeStruct(kall.shape, F32),
        jax.ShapeDtypeStruct(vall.shape, F32),
        jax.ShapeDtypeStruct((bsz, n_tiles, tm, d_attn), BF16),
        jax.ShapeDtypeStruct((bsz, n_tiles, d_attn, tm), BF16),
        jax.ShapeDtypeStruct((bsz, n_tiles, d_attn, tm), F32),
        jax.ShapeDtypeStruct((bsz, n_tiles, 1, d_attn), F32),
        jax.ShapeDtypeStruct((bsz, t, d_attn), F32),
        jax.ShapeDtypeStruct((bsz, t, d_model), F32),
        jax.ShapeDtypeStruct((bsz, t, d_model), F32),
        jax.ShapeDtypeStruct((bsz, POOL_HALO - 1, d_pool), F32),
        jax.ShapeDtypeStruct((bsz, CONV_W - 1, d_conv), F32),
    )
    tall = pl.BlockSpec((1, 1, d_attn, tm), lambda b, i: (b, layer, 0, i))
    out_specs = (
        tall, tall, blk(tm, d_attn), blk(d_attn, tm), blk(d_attn, tm), blk(1, d_attn),
        row(d_attn), row(d_model), row(d_model),
        pl.BlockSpec((1, POOL_HALO - 1, d_pool), lambda b, i: (b, 0, 0)),
        pl.BlockSpec((1, CONV_W - 1, d_conv), lambda b, i: (b, 0, 0)),
    )
    in_specs = ([row(d_model)] + [_const_spec(a.shape) for a in (win, wpool, spool, wdw, bdw, clng, clnb, wpw2, pa, pc)]
                + [pl.BlockSpec(memory_space=pl.ANY)] * 2)
    return pl.pallas_call(
        kern, grid=(bsz, n_tiles), in_specs=in_specs, out_specs=out_specs, out_shape=out_shape,
        input_output_aliases={11: 0, 12: 1},
        scratch_shapes=[pltpu.VMEM((POOL_HALO + tm, d_pool), F32), pltpu.VMEM((CONV_HALO + tm, d_conv), F32)],
        compiler_params=pltpu.CompilerParams(dimension_semantics=("arbitrary", "arbitrary"),
                                             vmem_limit_bytes=VMEM_LIMIT),
        name="proj_prompt",
    )(x, win, wpool, spool, wdw, bdw, clng, clnb, wpw2, pa, pc, kall, vall)


def _attn_prompt_kernel(rel_ref, qt_ref, kb_ref, vt_ref, kmean_ref, bko_ref, bkp_ref, x_ref, hac_ref, smb_ref,
                        gbs_ref, pb_ref, wout_ref, lng_ref, lnb_ref, o_ref,
                        bown_ref, bprev_ref, msk_ref, qb_ref, s_ref, p_ref, m_ref, l_ref, a_ref, acc_ref, yb_ref,
                        *, nb, tq, alpha, n_buckets):
    i = pl.program_id(1)

    @pl.when(jnp.logical_and(pl.program_id(0) == 0, i == 0))
    def _():
        for h in range(N_HEADS):
            bown_ref[h] = jnp.full((tq, tq), NEG, F32)
            bprev_ref[h] = jnp.zeros((tq, tq), F32)

            def fill(bkt, c):
                val = rel_ref[bkt, h] * LOG2E
                bown_ref[h] = jnp.where(bko_ref[...] == bkt, val, bown_ref[h])
                bprev_ref[h] = jnp.where(bkp_ref[...] == bkt, val, bprev_ref[h])
                return c

            lax.fori_loop(0, n_buckets, fill, 0)

    n_iota = lax.broadcasted_iota(jnp.int32, (nb, tq), 0)
    past = n_iota < i
    half = lax.broadcasted_iota(jnp.int32, (2 * HEAD_DIM, tq), 0) // HEAD_DIM
    pair = lambda h: slice((h // 2) * 2 * HEAD_DIM, (h // 2 + 1) * 2 * HEAD_DIM)
    rows = lambda h: slice(h * HEAD_DIM, (h + 1) * HEAD_DIM)

    for h in range(N_HEADS):
        qh = jnp.where(half == h % 2, qt_ref[0, 0, pair(h), :], 0.0)
        gate = jnp.dot(kmean_ref[0, :, pair(h)], qh, preferred_element_type=F32,
                       precision=lax.Precision.HIGHEST)
        sel = _top_k_mask(jnp.where(past, gate, NEG), n_iota, min(MOBA_TOPK, nb), nb)
        msk_ref[h] = jnp.where(jnp.logical_and(sel, past), 0.0, NEG)
        qb_ref[h] = (qh * (HEAD_DIM ** -0.5 * LOG2E)).astype(BF16)
        m_ref[h] = jnp.full((1, tq), NEG, F32)
        l_ref[h] = jnp.zeros((1, tq), F32)
        acc_ref[h] = jnp.zeros((HEAD_DIM, tq), F32)

    def scores(n, slot):
        for h in range(N_HEADS):
            s_ref[slot, h] = jnp.dot(kb_ref[0, n, :, pair(h)], qb_ref[h], preferred_element_type=F32)

    def softmax_pv(n, slot, bias_ref, shift_of):
        for h in range(N_HEADS):
            s = s_ref[slot, h] if bias_ref is None else s_ref[slot, h] + bias_ref[h]
            shift = shift_of(h)
            m_blk = jnp.max(s, axis=0, keepdims=True)
            m_old = m_ref[h]
            if shift is None:
                m_new = jnp.maximum(m_old, m_blk)
                p = jnp.exp2(s - m_new)
            else:
                m_new = jnp.maximum(m_old, m_blk + shift)
                p = jnp.exp2(s + (shift - m_new))
            a = jnp.exp2(m_old - m_new)
            l_ref[h] = a * l_ref[h] + jnp.sum(p, axis=0, keepdims=True)
            a_ref[h] = a
            m_ref[h] = m_new
            p_ref[h] = p.astype(BF16)
        for h in range(N_HEADS):
            acc_ref[h] = a_ref[h] * acc_ref[h] + jnp.dot(vt_ref[0, n, rows(h), :], p_ref[h],
                                                         preferred_element_type=F32)

    @pl.when(i == 0)
    def _():
        scores(i, 0)
        softmax_pv(i, 0, bown_ref, lambda h: None)

    @pl.when(i >= 1)
    def _():
        scores(i, 0)
        scores(i - 1, 1)
        softmax_pv(i, 0, bown_ref, lambda h: None)
        softmax_pv(i - 1, 1, bprev_ref, lambda h: msk_ref[h, pl.ds(i - 1, 1), :])

    n_far = jnp.maximum(i - 1, 0)

    def far_group(k, c):
        blocks = [FAR_UNROLL * k + u for u in range(FAR_UNROLL)]
        safe = [jnp.minimum(n, nb - 1) for n in blocks]
        for u in range(FAR_UNROLL):
            scores(safe[u], u)
        for u in range(FAR_UNROLL):
            softmax_pv(safe[u], u, None, lambda h, u=u: jnp.where(
                blocks[u] < n_far, msk_ref[h, pl.ds(safe[u], 1), :] + rel_ref[n_buckets - 1, h] * LOG2E, NEG))
        return c

    lax.fori_loop(0, (n_far + FAR_UNROLL - 1) // FAR_UNROLL, far_group, 0)

    for h in range(N_HEADS):
        o = acc_ref[h] / l_ref[h]
        yb_ref[:, rows(h)] = o.T

    o_ref[0] = _merge(yb_ref[...], gbs_ref[0], hac_ref[0], smb_ref[0], x_ref[0], pb_ref, wout_ref, lng_ref,
                      lnb_ref, alpha)


def _attn_prompt(rel_bias, qt, kb, vt, kmean, bko, bkp, x, hac, smb, gbs, pb, wout, lng, lnb, alpha):
    bsz, t, d_model = x.shape
    nb, tq = kb.shape[1], kb.shape[2]
    d_attn = kb.shape[3]
    kern = functools.partial(_attn_prompt_kernel, nb=nb, tq=tq, alpha=alpha, n_buckets=rel_bias.shape[0])
    row = lambda w: pl.BlockSpec((1, tq, w), lambda b, i: (b, i, 0))
    per_b = lambda a: pl.BlockSpec((1,) + a.shape[1:], lambda b, i: (b,) + (0,) * (a.ndim - 1),
                                   pipeline_mode=pl.Buffered(1))
    in_specs = [
        pl.BlockSpec(memory_space=pltpu.SMEM),
        pl.BlockSpec((1, 1, d_attn, tq), lambda b, i: (b, i, 0, 0)),
        per_b(kb), per_b(vt), per_b(kmean),
        _const_spec(bko.shape), _const_spec(bkp.shape),
        row(d_model), row(d_model), row(d_model), row(d_attn),
        _const_spec(pb.shape), _const_spec(wout.shape), _const_spec(lng.shape), _const_spec(lnb.shape),
    ]
    return pl.pallas_call(
        kern, grid=(bsz, nb), in_specs=in_specs, out_specs=row(d_model),
        out_shape=jax.ShapeDtypeStruct((bsz, t, d_model), F32),
        scratch_shapes=[pltpu.VMEM((N_HEADS, tq, tq), F32), pltpu.VMEM((N_HEADS, tq, tq), F32),
                        pltpu.VMEM((N_HEADS, nb, tq), F32), pltpu.VMEM((N_HEADS, 2 * HEAD_DIM, tq), BF16),
                        pltpu.VMEM((FAR_UNROLL, N_HEADS, tq, tq), F32), pltpu.VMEM((N_HEADS, tq, tq), BF16),
                        pltpu.VMEM((N_HEADS, 1, tq), F32), pltpu.VMEM((N_HEADS, 1, tq), F32),
                        pltpu.VMEM((N_HEADS, 1, tq), F32),
                        pltpu.VMEM((N_HEADS, HEAD_DIM, tq), F32), pltpu.VMEM((tq, d_attn), F32)],
        compiler_params=pltpu.CompilerParams(dimension_semantics=("arbitrary", "arbitrary"),
                                             vmem_limit_bytes=VMEM_LIMIT),
        name="attn_prompt",
    )(rel_bias, qt, kb, vt, kmean, bko, bkp, x, hac, smb, gbs, pb, wout, lng, lnb)


def _proj_sample_kernel(x_ref, pst_in_ref, cst_in_ref, win_ref, wpool_ref, spool_ref, wdw_ref, bdw_ref, clng_ref,
                        clnb_ref, wpw2_ref, pa_ref, pc_ref,
                        q_ref, k_ref, v_ref, gbs_ref, smb_ref, hac_ref, pst_ref, cst_ref,
                        extu_ref, exta_ref, *, tm, rs, sp, d_pool):
    xb = x_ref[...].astype(BF16)

    def seg(name):
        a, b = sp[name]
        return jnp.dot(xb, win_ref[:, a:b], preferred_element_type=F32)

    n_pool, n_conv = (POOL_HALO - 1) * rs, (CONV_W - 1) * rs
    extu_ref[0:rs, :] = jnp.zeros((rs, d_pool), F32)
    extu_ref[pl.ds(rs, n_pool), :] = pst_in_ref[...]
    exta_ref[0:(CONV_HALO - CONV_W + 1) * rs, :] = jnp.zeros(((CONV_HALO - CONV_W + 1) * rs, exta_ref.shape[1]), F32)
    exta_ref[pl.ds((CONV_HALO - CONV_W + 1) * rs, n_conv), :] = cst_in_ref[...]

    wl = jnp.zeros((1, d_pool), jnp.int32)
    lane = lax.broadcasted_iota(jnp.int32, (1, d_pool), 1)
    gdim = d_pool // len(POOL_WINDOWS)
    for g, w in enumerate(POOL_WINDOWS):
        wl = jnp.where(lane >= g * gdim, w, wl)
    cnt = wl.astype(F32)

    hac_ref[...] = _branches(seg, extu_ref, exta_ref, cnt, wpool_ref, spool_ref, wdw_ref, bdw_ref, clng_ref,
                             clnb_ref, wpw2_ref, pa_ref, pc_ref, tm=tm, rs=rs, d_pool=d_pool)
    pst_ref[...] = extu_ref[pl.ds(POOL_HALO * rs + tm - n_pool, n_pool), :]
    cst_ref[...] = exta_ref[pl.ds(CONV_HALO * rs + tm - n_conv, n_conv), :]
    q_ref[...] = seg("q")
    k_ref[...] = seg("k")
    v_ref[...] = seg("v")
    gbs_ref[...] = _silu(seg("g_b"))
    smb_ref[...] = _sigmoid(seg("m_b"))


def _proj_sample(x_tm, pst_tm, cst_tm, win, wpool, spool, wdw, bdw, clng, clnb, wpw2, pa, pc, sp, rs):
    tm, d_model = x_tm.shape
    d_pool, d_conv, d_attn = wpool.shape[0], wpw2.shape[0], sp["q"][1] - sp["q"][0]
    kern = functools.partial(_proj_sample_kernel, tm=tm, rs=rs, sp=sp, d_pool=d_pool)
    out_shape = (
        jax.ShapeDtypeStruct((tm, d_attn), F32), jax.ShapeDtypeStruct((tm, d_attn), F32),
        jax.ShapeDtypeStruct((tm, d_attn), F32), jax.ShapeDtypeStruct((tm, d_attn), F32),
        jax.ShapeDtypeStruct((tm, d_model), F32), jax.ShapeDtypeStruct((tm, d_model), F32),
        jax.ShapeDtypeStruct(pst_tm.shape, F32), jax.ShapeDtypeStruct(cst_tm.shape, F32),
    )
    return pl.pallas_call(
        kern, out_shape=out_shape,
        scratch_shapes=[pltpu.VMEM((POOL_HALO * rs + tm, d_pool), F32), pltpu.VMEM((CONV_HALO * rs + tm, d_conv), F32)],
        compiler_params=pltpu.CompilerParams(vmem_limit_bytes=VMEM_LIMIT),
        name="proj_sample",
    )(x_tm, pst_tm, cst_tm, win, wpool, spool, wdw, bdw, clng, clnb, wpw2, pa, pc)


def _attn_sample_kernel(pt_ref, rel_ref, *refs, nb, bps, tdec, n_buckets):
    npg = 2 * bps
    k_refs, v_refs = refs[:npg], refs[npg:2 * npg]
    (q_ref, knew_ref, vnew_ref, bks_ref, bkn_ref, o_ref,
     blast_ref, cfar_ref, bown_ref, ms_ref, ls_ref, gs_ref, r_ref) = refs[2 * npg:]
    jj = pl.program_id(1)
    scale = HEAD_DIM ** -0.5
    bnn = (((2,), (1,)), ((0,), (0,)))
    bnt = (((2,), (2,)), ((0,), (0,)))

    @pl.when(jnp.logical_and(pl.program_id(0) == 0, jj == 0))
    def _():
        for h in range(N_HEADS):
            blast_ref[h] = jnp.zeros(blast_ref.shape[1:], F32)
            bown_ref[h] = jnp.full(bown_ref.shape[1:], NEG, F32)
            cfar_ref[h] = jnp.full(cfar_ref.shape[1:], rel_ref[n_buckets - 1, h], F32)

            def fill(bkt, c):
                val = rel_ref[bkt, h]
                blast_ref[h] = jnp.where(bks_ref[...] == bkt, val, blast_ref[h])
                bown_ref[h] = jnp.where(bkn_ref[...] == bkt, val, bown_ref[h])
                return c

            lax.fori_loop(0, n_buckets, fill, 0)

    q3 = q_ref[0]
    qs = (q3 * scale).astype(BF16)

    raw = [jnp.concatenate(
        [lax.dot_general(qs, k_refs[2 * c + o][0, 0].astype(BF16), bnn, preferred_element_type=F32)
         for o in range(2)], axis=-1) for c in range(bps)]
    probs = []
    for c in range(bps):
        j = jj * bps + c
        gs_ref[j] = jnp.sum(raw[c], axis=-1, keepdims=True) * (1.0 / (MOBA_BLOCK * scale))
        if c == bps - 1:
            s = raw[c] + jnp.where(j == nb - 1, blast_ref[...], cfar_ref[...])
        else:
            s = raw[c] + cfar_ref[...]
        m = jnp.max(s, axis=-1, keepdims=True)
        p = jnp.exp(s - m)
        ms_ref[j] = m
        ls_ref[j] = jnp.sum(p, axis=-1, keepdims=True)
        probs.append(p.astype(BF16))
    for c in range(bps):
        pb = probs[c]
        half = pb.shape[-1] // 2
        r_ref[jj * bps + c] = (
            lax.dot_general(pb[..., :half], v_refs[2 * c][0, 0].astype(BF16), bnt, preferred_element_type=F32)
            + lax.dot_general(pb[..., half:], v_refs[2 * c + 1][0, 0].astype(BF16), bnt,
                              preferred_element_type=F32))

    @pl.when(jj == nb // bps - 1)
    def _():
        knew, vnew = knew_ref[0], vnew_ref[0]
        qf = q3 * scale
        s_o = jnp.concatenate([jnp.sum(qf * knew[:, t:t + 1, :], axis=-1, keepdims=True) for t in range(tdec)],
                              axis=-1) + bown_ref[...]
        m_o = jnp.max(s_o, axis=-1, keepdims=True)
        p_o = jnp.exp(s_o - m_o)
        l_o = jnp.sum(p_o, axis=-1, keepdims=True)
        r_o = p_o[..., 0:1] * vnew[:, 0:1, :]
        for t in range(1, tdec):
            r_o = r_o + p_o[..., t:t + 1] * vnew[:, t:t + 1, :]
        g = gs_ref[...]
        n_iota = lax.broadcasted_iota(jnp.int32, g.shape, 0)
        sel = _top_k_mask(g, n_iota, min(MOBA_TOPK, nb), nb)
        ms = jnp.where(sel, ms_ref[...], NEG)
        mtot = jnp.maximum(jnp.max(ms, axis=0), m_o)
        w = jnp.where(sel, jnp.exp(ms - mtot[None]), 0.0)
        w_o = jnp.exp(m_o - mtot)
        ltot = w_o * l_o + jnp.sum(w * ls_ref[...], axis=0)
        acc = w_o * r_o + jnp.sum(w * r_ref[...], axis=0)
        o_ref[0] = acc / ltot


def _attn_sample(page_table, rel_bias, cache_kt, cache_vt, layer, q3, knew, vnew, bks, bkn):
    dec_b, _, qp, _ = q3.shape
    tdec = knew.shape[2]
    page = cache_kt.shape[4]
    assert 2 * page == MOBA_BLOCK, "one MoBA block is swept as two cache pages"
    nb = page_table.shape[1] // 2
    bps = math.gcd(SAMPLE_BLOCKS_PER_STEP, nb)
    kern = functools.partial(_attn_sample_kernel, nb=nb, bps=bps, tdec=tdec, n_buckets=rel_bias.shape[0])
    pg = lambda o: pl.BlockSpec((1, 1, N_HEADS, HEAD_DIM, page),
                                lambda b, jj, pt: (pt[b, 2 * bps * jj + o], layer, 0, 0, 0))
    full = lambda a: pl.BlockSpec(a.shape, lambda b, jj, pt: (0,) * a.ndim)
    per_b = lambda a: pl.BlockSpec((1,) + a.shape[1:], lambda b, jj, pt: (b,) + (0,) * (a.ndim - 1))
    stat = pltpu.VMEM((nb, N_HEADS, qp, 1), F32)
    grid_spec = pltpu.PrefetchScalarGridSpec(
        num_scalar_prefetch=1, grid=(dec_b, nb // bps),
        in_specs=([pl.BlockSpec(memory_space=pltpu.SMEM)] + [pg(o) for o in range(2 * bps)] * 2
                  + [per_b(q3), per_b(knew), per_b(vnew), full(bks), full(bkn)]),
        out_specs=pl.BlockSpec((1, N_HEADS, qp, HEAD_DIM), lambda b, jj, pt: (b, 0, 0, 0)),
        scratch_shapes=[pltpu.VMEM((N_HEADS,) + bks.shape, F32), pltpu.VMEM((N_HEADS,) + bks.shape, F32),
                        pltpu.VMEM((N_HEADS,) + bkn.shape, F32), stat, stat, stat,
                        pltpu.VMEM((nb, N_HEADS, qp, HEAD_DIM), F32)],
    )
    return pl.pallas_call(
        kern, grid_spec=grid_spec, out_shape=jax.ShapeDtypeStruct((dec_b, N_HEADS, qp, HEAD_DIM), F32),
        compiler_params=pltpu.CompilerParams(dimension_semantics=("arbitrary", "arbitrary"),
                                             vmem_limit_bytes=VMEM_LIMIT),
        name="attn_sample",
    )(page_table, rel_bias, *([cache_kt] * (2 * bps)), *([cache_vt] * (2 * bps)), q3, knew, vnew, bks, bkn)


def _merge_sample_kernel(yb_ref, gbs_ref, hac_ref, smb_ref, x_ref, pb_ref, wout_ref, lng_ref, lnb_ref, o_ref, *, alpha):
    o_ref[...] = _merge(yb_ref[...], gbs_ref[...], hac_ref[...], smb_ref[...], x_ref[...], pb_ref, wout_ref,
                        lng_ref, lnb_ref, alpha)


def _merge_sample(yb, gbs, hac, smb, x, pb, wout, lng, lnb, alpha):
    return pl.pallas_call(
        functools.partial(_merge_sample_kernel, alpha=alpha),
        out_shape=jax.ShapeDtypeStruct(x.shape, F32),
        compiler_params=pltpu.CompilerParams(vmem_limit_bytes=VMEM_LIMIT),
        name="merge_sample",
    )(yb, gbs, hac, smb, x, pb, wout, lng, lnb)


def _block_diag(w):
    g, c, d = w.shape
    out = jnp.zeros((g * c, g * d), w.dtype)
    for i in range(g):
        out = out.at[i * c:(i + 1) * c, i * d:(i + 1) * d].set(w[i])
    return out


def kernel(x_prompt, x_sample, cache_k, cache_v, page_table, state_pool, state_conv, rel_bias, w_in, w_pool, s_pool,
           w_dw, b_dw, conv_ln_g, conv_ln_b, w_pw2, p_a, p_b, p_c, w_out, ln_g, ln_b):
    depth = w_in.shape[0]
    bsz, t, d_model = x_prompt.shape
    dec_b, tdec, _ = x_sample.shape
    d_pool = w_pool.shape[1] * w_pool.shape[2]
    d_attn, d_conv = p_b.shape[1], p_c.shape[1]
    assert d_attn == N_HEADS * HEAD_DIM and t % MOBA_BLOCK == 0
    page = cache_k.shape[2]
    past_len = page_table.shape[1] * page
    assert past_len % MOBA_BLOCK == 0 and past_len >= POOL_HALO
    sp = _splits(d_pool, d_attn, d_conv, d_model)
    alpha = (2 * depth) ** 0.25
    n_buckets = rel_bias.shape[0]

    bucket = _bucket_table(n_buckets, 2 * MOBA_BLOCK + tdec)
    far_rel = int(np.argmax(bucket == n_buckets - 1))
    assert (bucket[far_rel:] == n_buckets - 1).all() and far_rel <= MOBA_BLOCK
    d_own = np.arange(MOBA_BLOCK)[None, :] - np.arange(MOBA_BLOCK)[:, None]
    bko = jnp.asarray(np.where(d_own >= 0, bucket[np.maximum(d_own, 0)], -1).astype(np.int32))
    bkp = jnp.asarray(bucket[d_own + MOBA_BLOCK].astype(np.int32))
    qp = -(-tdec // 8) * 8
    dq = np.minimum(np.arange(qp), tdec - 1)[:, None]
    bks = jnp.asarray(bucket[MOBA_BLOCK + dq - np.arange(MOBA_BLOCK)[None, :]].astype(np.int32))
    d_new = dq - np.arange(tdec)[None, :]
    bkn = jnp.asarray(np.where(d_new >= 0, bucket[np.maximum(d_new, 0)], -1).astype(np.int32))
    rel = rel_bias.astype(F32)
    cache_kt = cache_k.transpose(0, 1, 3, 4, 2)
    cache_vt = cache_v.transpose(0, 1, 3, 4, 2)
    kall = jnp.zeros((bsz, depth, d_attn, t), F32)
    vall = jnp.zeros((bsz, depth, d_attn, t), F32)

    xp = x_prompt
    xs = x_sample.transpose(1, 0, 2).reshape(tdec * dec_b, d_model)
    outs = {n: [] for n in ("ks", "vs", "pp", "ps", "cp", "cs")}
    for l in range(depth):
        win = w_in[l].astype(BF16)
        wpool = _block_diag(w_pool[l]).astype(BF16)
        wl = (win, wpool, s_pool[l][None], w_dw[l], b_dw[l][None], conv_ln_g[l][None], conv_ln_b[l][None],
              w_pw2[l].astype(BF16), p_a[l].astype(BF16), p_c[l].astype(BF16))
        pb, wout, lng, lnb = p_b[l].astype(BF16), w_out[l].astype(BF16), ln_g[l][None], ln_b[l][None]

        kall, vall, kb, vt, qt, kmean, gbs, smb, hac, pst, cst = _proj_prompt(xp, *wl, kall, vall, l, sp)
        xp = _attn_prompt(rel, qt, kb, vt, kmean.reshape(bsz, -1, d_attn), bko, bkp, xp, hac, smb, gbs, pb, wout,
                          lng, lnb, alpha)
        outs["pp"].append(pst)
        outs["cp"].append(cst)

        pst_tm = state_pool[:, l].transpose(1, 0, 2).reshape(-1, d_pool)
        cst_tm = state_conv[:, l].transpose(1, 0, 2).reshape(-1, d_conv)
        q_s, k_s, v_s, gbs_s, smb_s, hac_s, pst_s, cst_s = _proj_sample(xs, pst_tm, cst_tm, *wl, sp, dec_b)
        to_h = lambda a: a.reshape(tdec, dec_b, N_HEADS, HEAD_DIM).transpose(1, 2, 0, 3)
        q_h, k_h, v_h = to_h(q_s), to_h(k_s), to_h(v_s)
        q3 = jnp.pad(q_h, ((0, 0), (0, 0), (0, qp - tdec), (0, 0)))
        yb_s = _attn_sample(page_table, rel, cache_kt, cache_vt, l, q3, k_h, v_h, bks, bkn)
        yb_tm = yb_s[:, :, :tdec].transpose(2, 0, 1, 3).reshape(tdec * dec_b, d_attn)
        k_b, v_b = k_h.transpose(0, 2, 1, 3), v_h.transpose(0, 2, 1, 3)
        xs = _merge_sample(yb_tm, gbs_s, hac_s, smb_s, xs, pb, wout, lng, lnb, alpha)
        outs["ks"].append(k_b)
        outs["vs"].append(v_b)
        outs["ps"].append(pst_s.reshape(-1, dec_b, d_pool).transpose(1, 0, 2))
        outs["cs"].append(cst_s.reshape(-1, dec_b, d_conv).transpose(1, 0, 2))

    st = lambda n: jnp.stack(outs[n], axis=1)
    y_sample = xs.reshape(tdec, dec_b, d_model).transpose(1, 0, 2)
    to_out = lambda a: a.reshape(bsz, depth, N_HEADS, HEAD_DIM, t).transpose(0, 1, 4, 2, 3)
    return (xp, y_sample, to_out(kall), to_out(vall), st("ks"), st("vs"), st("pp"), st("ps"), st("cp"), st("cs"))
```

```python
import functools
import math

import numpy as np
import jax
import jax.numpy as jnp
from jax import lax
from jax.experimental import pallas as pl
from jax.experimental.pallas import tpu as pltpu

POOL_WINDOWS = (2, 4, 8, 16)
POOL_HALO = 16
CONV_W = 31
CONV_HALO = 32
N_HEADS = 8
HEAD_DIM = 64
MOBA_BLOCK = 256
MOBA_TOPK = 3
REL_MAX_DIST = 128
LN_EPS = 1e-5
NEG = -1e30
LOG2E = 1.4426950408889634
LANES = 128
VMEM_LIMIT = 56 * 1024 * 1024
SAMPLE_BLOCKS_PER_STEP = 8
FAR_UNROLL = 4
ONES_ROWS = 16
VT_ROWS = HEAD_DIM + ONES_ROWS

F32 = jnp.float32
BF16 = jnp.bfloat16


def _sigmoid(x):
    return 1.0 / (1.0 + jnp.exp(-x))


def _silu(x):
    return x * _sigmoid(x)


def _layer_norm(x, g, b):
    mu = jnp.mean(x, axis=-1, keepdims=True)
    xc = x - mu
    var = jnp.mean(xc * xc, axis=-1, keepdims=True)
    return xc * lax.rsqrt(var + LN_EPS) * g + b


def _bucket_table(n_buckets, n):
    rel = np.arange(n)
    max_exact = n_buckets // 2
    nf = np.maximum(rel, 1).astype(np.float32)
    large = max_exact + (np.log(nf / np.float32(max_exact)) / np.float32(math.log(REL_MAX_DIST / max_exact))
                         * np.float32(n_buckets - max_exact)).astype(np.int32)
    large = np.minimum(large, n_buckets - 1)
    return np.where(rel < max_exact, rel, large).astype(np.int32)


def _splits(d_pool, d_attn, d_conv, d_model):
    names = ("u_a", "g_a", "q", "k", "v", "g_b", "c_val", "c_gate", "g_c", "m_a", "m_b", "m_c")
    widths = (d_pool, d_pool, d_attn, d_attn, d_attn, d_attn, d_conv, d_conv, d_conv, d_model, d_model, d_model)
    off, out = 0, {}
    for nme, w in zip(names, widths):
        out[nme] = (off, off + w)
        off += w
    return out


def _top_k_mask(g, n_iota, k, n):
    sel = jnp.zeros(g.shape, jnp.bool_)
    for _ in range(k):
        mx = jnp.max(g, axis=0, keepdims=True)
        idx = jnp.min(jnp.where(g == mx, n_iota, n), axis=0, keepdims=True)
        pick = n_iota == idx
        sel = jnp.logical_or(sel, pick)
        g = jnp.where(pick, -jnp.inf, g)
    return sel


def _branches(seg, extu_ref, exta_ref, cnt, wpool_ref, spool_ref, wdw_ref, bdw_ref, clng_ref, clnb_ref,
              wpw2_ref, pa_ref, pc_ref, *, tm, rs, d_pool, between=None):
    extu_ref[pl.ds(POOL_HALO * rs, tm), :] = seg("u_a")
    exta_ref[pl.ds(CONV_HALO * rs, tm), :] = seg("c_val") * _sigmoid(seg("c_gate"))
    if between is not None:
        between[0]()

    u = extu_ref[pl.ds(POOL_HALO * rs, tm), :]
    gdim = d_pool // len(POOL_WINDOWS)
    lane = lax.broadcasted_iota(jnp.int32, (1, d_pool), 1)
    acc = u
    num = None
    j = 1
    for g, w in enumerate(POOL_WINDOWS):
        while j < w:
            acc = acc + extu_ref[pl.ds((POOL_HALO - j) * rs, tm), :]
            j += 1
        num = acc if num is None else jnp.where(lane >= g * gdim, acc, num)
    pooled = num / cnt
    r = pooled - u
    ya = jnp.dot(r.astype(BF16), wpool_ref[...], preferred_element_type=F32) * spool_ref[...]
    ya = ya * _silu(seg("g_a"))
    hac = _sigmoid(seg("m_a")) * jnp.dot(ya.astype(BF16), pa_ref[...], preferred_element_type=F32)
    if between is not None:
        between[1]()

    h = jnp.zeros((tm, exta_ref.shape[1]), F32) + bdw_ref[...]
    for j in range(CONV_W):
        h = h + wdw_ref[j:j + 1, :] * exta_ref[pl.ds((CONV_HALO - (CONV_W - 1) + j) * rs, tm), :]
    h = _silu(_layer_norm(h, clng_ref[...], clnb_ref[...]))
    yc = jnp.dot(h.astype(BF16), wpw2_ref[...], preferred_element_type=F32) * _silu(seg("g_c"))
    hac = hac + _sigmoid(seg("m_c")) * jnp.dot(yc.astype(BF16), pc_ref[...], preferred_element_type=F32)
    return hac


def _merge(yb, gbs, hac, smb, x, pb_ref, wout_ref, lng_ref, lnb_ref, alpha):
    ybg = (yb * gbs).astype(BF16)
    h = hac + smb * jnp.dot(ybg, pb_ref[...], preferred_element_type=F32)
    y = alpha * x + jnp.dot(h.astype(BF16), wout_ref[...], preferred_element_type=F32)
    return _layer_norm(y, lng_ref[...], lnb_ref[...])


def _proj_prompt_kernel(x_ref, win_ref, wpool_ref, spool_ref, wdw_ref, bdw_ref, clng_ref, clnb_ref, wpw2_ref,
                        pa_ref, pc_ref, kall_in_ref, vall_in_ref,
                        kall_ref, vall_ref, kb_ref, vt_ref, qt_ref, kmean_ref, gbs_ref, smb_ref, hac_ref, pst_ref,
                        cst_ref, extu_ref, exta_ref, *, tm, sp, d_pool, n_tiles):
    del kall_in_ref, vall_in_ref
    i = pl.program_id(1)
    xb = x_ref[0].astype(BF16)

    def seg(name):
        a, b = sp[name]
        return jnp.dot(xb, win_ref[:, a:b], preferred_element_type=F32)

    @pl.when(i == 0)
    def _():
        extu_ref[0:POOL_HALO, :] = jnp.zeros((POOL_HALO, d_pool), F32)
        exta_ref[0:CONV_HALO, :] = jnp.zeros((CONV_HALO, exta_ref.shape[1]), F32)

    pos1 = i * tm + lax.broadcasted_iota(jnp.int32, (tm, 1), 0) + 1
    wl = jnp.zeros((1, d_pool), jnp.int32)
    lane = lax.broadcasted_iota(jnp.int32, (1, d_pool), 1)
    gdim = d_pool // len(POOL_WINDOWS)
    for g, w in enumerate(POOL_WINDOWS):
        wl = jnp.where(lane >= g * gdim, w, wl)
    cnt = jnp.minimum(pos1, wl).astype(F32)

    def attention_inputs():
        q = seg("q")
        qt_ref[0, 0] = q.T
        k = seg("k")
        kall_ref[0, 0] = k.T
        kb_ref[0, 0] = k.astype(BF16)
        kmean_ref[0, 0] = jnp.sum(k, axis=0, keepdims=True) * (1.0 / tm)
        vt = seg("v").T
        vall_ref[0, 0] = vt
        vtb = vt.astype(BF16)
        for h in range(N_HEADS):
            vt_ref[0, 0, h * VT_ROWS:h * VT_ROWS + HEAD_DIM, :] = vtb[h * HEAD_DIM:(h + 1) * HEAD_DIM]
            vt_ref[0, 0, h * VT_ROWS + HEAD_DIM:(h + 1) * VT_ROWS, :] = jnp.ones((ONES_ROWS, tm), BF16)

    def attention_gates():
        gbs_ref[0] = _silu(seg("g_b"))
        smb_ref[0] = _sigmoid(seg("m_b"))

    hac_ref[0] = _branches(seg, extu_ref, exta_ref, cnt, wpool_ref, spool_ref, wdw_ref, bdw_ref, clng_ref,
                           clnb_ref, wpw2_ref, pa_ref, pc_ref, tm=tm, rs=1, d_pool=d_pool,
                           between=(attention_inputs, attention_gates))

    @pl.when(i == n_tiles - 1)
    def _():
        pst_ref[0] = extu_ref[pl.ds(tm + 1, POOL_HALO - 1), :]
        cst_ref[0] = exta_ref[pl.ds(tm + CONV_HALO - (CONV_W - 1), CONV_W - 1), :]

    extu_ref[0:POOL_HALO, :] = extu_ref[pl.ds(tm, POOL_HALO), :]
    exta_ref[0:CONV_HALO, :] = exta_ref[pl.ds(tm, CONV_HALO), :]


def _const_spec(shape):
    nd = len(shape)
    return pl.BlockSpec(shape, lambda *_: (0,) * nd, pipeline_mode=pl.Buffered(1))


def _proj_prompt(x, win, wpool, spool, wdw, bdw, clng, clnb, wpw2, pa, pc, kall, vall, layer, sp):
    bsz, t, d_model = x.shape
    tm = MOBA_BLOCK
    n_tiles = t // tm
    d_pool, d_conv, d_attn = wpool.shape[0], wpw2.shape[0], sp["q"][1] - sp["q"][0]
    kern = functools.partial(_proj_prompt_kernel, tm=tm, sp=sp, d_pool=d_pool, n_tiles=n_tiles)
    row = lambda w: pl.BlockSpec((1, tm, w), lambda b, i: (b, i, 0))
    blk = lambda r, c: pl.BlockSpec((1, 1, r, c), lambda b, i: (b, i, 0, 0))
    out_shape = (
        jax.ShapeDtypeStruct(kall.shape, F32),
        jax.ShapeDtypeStruct(vall.shape, F32),
        jax.ShapeDtypeStruct((bsz, n_tiles, tm, d_attn), BF16),
        jax.ShapeDtypeStruct((bsz, n_tiles, N_HEADS * VT_ROWS, tm), BF16),
        jax.ShapeDtypeStruct((bsz, n_tiles, d_attn, tm), F32),
        jax.ShapeDtypeStruct((bsz, n_tiles, 1, d_attn), F32),
        jax.ShapeDtypeStruct((bsz, t, d_attn), F32),
        jax.ShapeDtypeStruct((bsz, t, d_model), F32),
        jax.ShapeDtypeStruct((bsz, t, d_model), F32),
        jax.ShapeDtypeStruct((bsz, POOL_HALO - 1, d_pool), F32),
        jax.ShapeDtypeStruct((bsz, CONV_W - 1, d_conv), F32),
    )
    tall = pl.BlockSpec((1, 1, d_attn, tm), lambda b, i: (b, layer, 0, i))
    out_specs = (
        tall, tall, blk(tm, d_attn), blk(N_HEADS * VT_ROWS, tm), blk(d_attn, tm), blk(1, d_attn),
        row(d_attn), row(d_model), row(d_model),
        pl.BlockSpec((1, POOL_HALO - 1, d_pool), lambda b, i: (b, 0, 0)),
        pl.BlockSpec((1, CONV_W - 1, d_conv), lambda b, i: (b, 0, 0)),
    )
    in_specs = ([row(d_model)] + [_const_spec(a.shape) for a in (win, wpool, spool, wdw, bdw, clng, clnb, wpw2, pa, pc)]
                + [pl.BlockSpec(memory_space=pl.ANY)] * 2)
    return pl.pallas_call(
        kern, grid=(bsz, n_tiles), in_specs=in_specs, out_specs=out_specs, out_shape=out_shape,
        input_output_aliases={11: 0, 12: 1},
        scratch_shapes=[pltpu.VMEM((POOL_HALO + tm, d_pool), F32), pltpu.VMEM((CONV_HALO + tm, d_conv), F32)],
        compiler_params=pltpu.CompilerParams(dimension_semantics=("arbitrary", "arbitrary"),
                                             vmem_limit_bytes=VMEM_LIMIT),
        name="proj_prompt",
    )(x, win, wpool, spool, wdw, bdw, clng, clnb, wpw2, pa, pc, kall, vall)


def _attn_prompt_kernel(rel_ref, qt_ref, kb_ref, vt_ref, kmean_ref, bko_ref, bkp_ref, x_ref, hac_ref, smb_ref,
                        gbs_ref, pb_ref, wout_ref, lng_ref, lnb_ref, o_ref,
                        bown_ref, bprev_ref, msk_ref, qb_ref, s_ref, mb_ref, p_ref, m_ref, l_ref, a_ref, acc_ref, yb_ref,
                        *, nb, tq, alpha, n_buckets):
    i = pl.program_id(1)

    @pl.when(jnp.logical_and(pl.program_id(0) == 0, i == 0))
    def _():
        for h in range(N_HEADS):
            bown_ref[h] = jnp.full((tq, tq), NEG, F32)
            bprev_ref[h] = jnp.zeros((tq, tq), F32)

            def fill(bkt, c):
                val = rel_ref[bkt, h] * LOG2E
                bown_ref[h] = jnp.where(bko_ref[...] == bkt, val, bown_ref[h])
                bprev_ref[h] = jnp.where(bkp_ref[...] == bkt, val, bprev_ref[h])
                return c

            lax.fori_loop(0, n_buckets, fill, 0)

    n_iota = lax.broadcasted_iota(jnp.int32, (nb, tq), 0)
    past = n_iota < i
    half = lax.broadcasted_iota(jnp.int32, (2 * HEAD_DIM, tq), 0) // HEAD_DIM
    pair = lambda h: slice((h // 2) * 2 * HEAD_DIM, (h // 2 + 1) * 2 * HEAD_DIM)
    rows = lambda h: slice(h * HEAD_DIM, (h + 1) * HEAD_DIM)

    for h in range(N_HEADS):
        qh = jnp.where(half == h % 2, qt_ref[0, 0, pair(h), :], 0.0)
        gate = jnp.dot(kmean_ref[0, :, pair(h)].astype(BF16), qh.astype(BF16),
                       preferred_element_type=F32)
        sel = _top_k_mask(jnp.where(past, gate, NEG), n_iota, min(MOBA_TOPK, nb), nb)
        msk_ref[h] = jnp.where(jnp.logical_and(sel, past), 0.0, NEG)
        qb_ref[h] = (qh * (HEAD_DIM ** -0.5 * LOG2E)).astype(BF16)
        m_ref[h] = jnp.full((1, tq), NEG, F32)
        l_ref[h] = jnp.zeros((1, tq), F32)
        acc_ref[h] = jnp.zeros((HEAD_DIM, tq), F32)

    def scores(n, slot, with_max):
        for h in range(N_HEADS):
            s = jnp.dot(kb_ref[0, n, :, pair(h)], qb_ref[h], preferred_element_type=F32)
            s_ref[slot, h] = s
            if with_max:
                mb_ref[slot, h] = jnp.max(s, axis=0, keepdims=True)

    def softmax_pv(n, slot, bias_ref, shift_of):
        for h in range(N_HEADS):
            m_old = m_ref[h]
            if bias_ref is None:
                s = s_ref[slot, h]
                shift = shift_of(h)
                m_new = jnp.maximum(m_old, mb_ref[slot, h] + shift)
                t = s + (shift - m_new)
            else:
                s = s_ref[slot, h] + bias_ref[h]
                shift = shift_of(h)
                m_blk = jnp.max(s, axis=0, keepdims=True)
                m_new = jnp.maximum(m_old, m_blk if shift is None else m_blk + shift)
                t = s - m_new if shift is None else s + (shift - m_new)
            a_ref[h] = jnp.exp2(m_old - m_new)
            m_ref[h] = m_new
            p_ref[h] = jnp.exp2(t.astype(BF16))
        for h in range(N_HEADS):
            pv = jnp.dot(vt_ref[0, n, h * VT_ROWS:(h + 1) * VT_ROWS, :], p_ref[h], preferred_element_type=F32)
            acc_ref[h] = a_ref[h] * acc_ref[h] + pv[:HEAD_DIM]
            l_ref[h] = a_ref[h] * l_ref[h] + pv[HEAD_DIM:HEAD_DIM + 1]

    @pl.when(i == 0)
    def _():
        scores(i, 0, False)
        softmax_pv(i, 0, bown_ref, lambda h: None)

    @pl.when(i >= 1)
    def _():
        scores(i, 0, False)
        scores(i - 1, 1, False)
        softmax_pv(i, 0, bown_ref, lambda h: None)
        softmax_pv(i - 1, 1, bprev_ref, lambda h: msk_ref[h, pl.ds(i - 1, 1), :])

    n_far = jnp.maximum(i - 1, 0)

    def far_group(k, c):
        blocks = [FAR_UNROLL * k + u for u in range(FAR_UNROLL)]
        safe = [jnp.minimum(n, nb - 1) for n in blocks]
        for u in range(FAR_UNROLL):
            scores(safe[u], u, True)
        for u in range(FAR_UNROLL):
            softmax_pv(safe[u], u, None, lambda h, u=u: jnp.where(
                blocks[u] < n_far, msk_ref[h, pl.ds(safe[u], 1), :] + rel_ref[n_buckets - 1, h] * LOG2E, NEG))
        return c

    lax.fori_loop(0, (n_far + FAR_UNROLL - 1) // FAR_UNROLL, far_group, 0)

    for h in range(N_HEADS):
        o = acc_ref[h] / l_ref[h]
        yb_ref[:, rows(h)] = o.T

    o_ref[0] = _merge(yb_ref[...], gbs_ref[0], hac_ref[0], smb_ref[0], x_ref[0], pb_ref, wout_ref, lng_ref,
                      lnb_ref, alpha)


def _attn_prompt(rel_bias, qt, kb, vt, kmean, bko, bkp, x, hac, smb, gbs, pb, wout, lng, lnb, alpha):
    bsz, t, d_model = x.shape
    nb, tq = kb.shape[1], kb.shape[2]
    d_attn = kb.shape[3]
    kern = functools.partial(_attn_prompt_kernel, nb=nb, tq=tq, alpha=alpha, n_buckets=rel_bias.shape[0])
    row = lambda w: pl.BlockSpec((1, tq, w), lambda b, i: (b, i, 0))
    per_b = lambda a: pl.BlockSpec((1,) + a.shape[1:], lambda b, i: (b,) + (0,) * (a.ndim - 1),
                                   pipeline_mode=pl.Buffered(1))
    in_specs = [
        pl.BlockSpec(memory_space=pltpu.SMEM),
        pl.BlockSpec((1, 1, d_attn, tq), lambda b, i: (b, i, 0, 0)),
        per_b(kb), per_b(vt), per_b(kmean),
        _const_spec(bko.shape), _const_spec(bkp.shape),
        row(d_model), row(d_model), row(d_model), row(d_attn),
        _const_spec(pb.shape), _const_spec(wout.shape), _const_spec(lng.shape), _const_spec(lnb.shape),
    ]
    return pl.pallas_call(
        kern, grid=(bsz, nb), in_specs=in_specs, out_specs=row(d_model),
        out_shape=jax.ShapeDtypeStruct((bsz, t, d_model), F32),
        scratch_shapes=[pltpu.VMEM((N_HEADS, tq, tq), F32), pltpu.VMEM((N_HEADS, tq, tq), F32),
                        pltpu.VMEM((N_HEADS, nb, tq), F32), pltpu.VMEM((N_HEADS, 2 * HEAD_DIM, tq), BF16),
                        pltpu.VMEM((FAR_UNROLL, N_HEADS, tq, tq), F32), pltpu.VMEM((FAR_UNROLL, N_HEADS, 1, tq), F32),
                        pltpu.VMEM((N_HEADS, tq, tq), BF16),
                        pltpu.VMEM((N_HEADS, 1, tq), F32), pltpu.VMEM((N_HEADS, 1, tq), F32),
                        pltpu.VMEM((N_HEADS, 1, tq), F32),
                        pltpu.VMEM((N_HEADS, HEAD_DIM, tq), F32), pltpu.VMEM((tq, d_attn), F32)],
        compiler_params=pltpu.CompilerParams(dimension_semantics=("arbitrary", "arbitrary"),
                                             vmem_limit_bytes=VMEM_LIMIT),
        name="attn_prompt",
    )(rel_bias, qt, kb, vt, kmean, bko, bkp, x, hac, smb, gbs, pb, wout, lng, lnb)


def _proj_sample_kernel(x_ref, pst_in_ref, cst_in_ref, win_ref, wpool_ref, spool_ref, wdw_ref, bdw_ref, clng_ref,
                        clnb_ref, wpw2_ref, pa_ref, pc_ref,
                        q_ref, k_ref, v_ref, gbs_ref, smb_ref, hac_ref, pst_ref, cst_ref,
                        extu_ref, exta_ref, *, tm, rs, sp, d_pool):
    xb = x_ref[...].astype(BF16)

    def seg(name):
        a, b = sp[name]
        return jnp.dot(xb, win_ref[:, a:b], preferred_element_type=F32)

    n_pool, n_conv = (POOL_HALO - 1) * rs, (CONV_W - 1) * rs
    extu_ref[0:rs, :] = jnp.zeros((rs, d_pool), F32)
    extu_ref[pl.ds(rs, n_pool), :] = pst_in_ref[...]
    exta_ref[0:(CONV_HALO - CONV_W + 1) * rs, :] = jnp.zeros(((CONV_HALO - CONV_W + 1) * rs, exta_ref.shape[1]), F32)
    exta_ref[pl.ds((CONV_HALO - CONV_W + 1) * rs, n_conv), :] = cst_in_ref[...]

    wl = jnp.zeros((1, d_pool), jnp.int32)
    lane = lax.broadcasted_iota(jnp.int32, (1, d_pool), 1)
    gdim = d_pool // len(POOL_WINDOWS)
    for g, w in enumerate(POOL_WINDOWS):
        wl = jnp.where(lane >= g * gdim, w, wl)
    cnt = wl.astype(F32)

    hac_ref[...] = _branches(seg, extu_ref, exta_ref, cnt, wpool_ref, spool_ref, wdw_ref, bdw_ref, clng_ref,
                             clnb_ref, wpw2_ref, pa_ref, pc_ref, tm=tm, rs=rs, d_pool=d_pool)
    pst_ref[...] = extu_ref[pl.ds(POOL_HALO * rs + tm - n_pool, n_pool), :]
    cst_ref[...] = exta_ref[pl.ds(CONV_HALO * rs + tm - n_conv, n_conv), :]
    q_ref[...] = seg("q")
    k_ref[...] = seg("k")
    v_ref[...] = seg("v")
    gbs_ref[...] = _silu(seg("g_b"))
    smb_ref[...] = _sigmoid(seg("m_b"))


def _proj_sample(x_tm, pst_tm, cst_tm, win, wpool, spool, wdw, bdw, clng, clnb, wpw2, pa, pc, sp, rs):
    tm, d_model = x_tm.shape
    d_pool, d_conv, d_attn = wpool.shape[0], wpw2.shape[0], sp["q"][1] - sp["q"][0]
    kern = functools.partial(_proj_sample_kernel, tm=tm, rs=rs, sp=sp, d_pool=d_pool)
    out_shape = (
        jax.ShapeDtypeStruct((tm, d_attn), F32), jax.ShapeDtypeStruct((tm, d_attn), F32),
        jax.ShapeDtypeStruct((tm, d_attn), F32), jax.ShapeDtypeStruct((tm, d_attn), F32),
        jax.ShapeDtypeStruct((tm, d_model), F32), jax.ShapeDtypeStruct((tm, d_model), F32),
        jax.ShapeDtypeStruct(pst_tm.shape, F32), jax.ShapeDtypeStruct(cst_tm.shape, F32),
    )
    return pl.pallas_call(
        kern, out_shape=out_shape,
        scratch_shapes=[pltpu.VMEM((POOL_HALO * rs + tm, d_pool), F32), pltpu.VMEM((CONV_HALO * rs + tm, d_conv), F32)],
        compiler_params=pltpu.CompilerParams(vmem_limit_bytes=VMEM_LIMIT),
        name="proj_sample",
    )(x_tm, pst_tm, cst_tm, win, wpool, spool, wdw, bdw, clng, clnb, wpw2, pa, pc)


def _attn_sample_kernel(pt_ref, rel_ref, *refs, nb, bps, tdec, n_buckets):
    npg = 2 * bps
    k_refs, v_refs = refs[:npg], refs[npg:2 * npg]
    (q_ref, knew_ref, vnew_ref, bks_ref, bkn_ref, o_ref,
     blast_ref, cfar_ref, bown_ref, ms_ref, ls_ref, gs_ref, r_ref) = refs[2 * npg:]
    jj = pl.program_id(1)
    scale = HEAD_DIM ** -0.5
    bnn = (((2,), (1,)), ((0,), (0,)))
    bnt = (((2,), (2,)), ((0,), (0,)))

    @pl.when(jnp.logical_and(pl.program_id(0) == 0, jj == 0))
    def _():
        for h in range(N_HEADS):
            blast_ref[h] = jnp.zeros(blast_ref.shape[1:], F32)
            bown_ref[h] = jnp.full(bown_ref.shape[1:], NEG, F32)
            cfar_ref[h] = jnp.full(cfar_ref.shape[1:], rel_ref[n_buckets - 1, h], F32)

            def fill(bkt, c):
                val = rel_ref[bkt, h]
                blast_ref[h] = jnp.where(bks_ref[...] == bkt, val, blast_ref[h])
                bown_ref[h] = jnp.where(bkn_ref[...] == bkt, val, bown_ref[h])
                return c

            lax.fori_loop(0, n_buckets, fill, 0)

    q3 = q_ref[0]
    qs = (q3 * scale).astype(BF16)

    raw = [jnp.concatenate(
        [lax.dot_general(qs, k_refs[2 * c + o][0, 0].astype(BF16), bnn, preferred_element_type=F32)
         for o in range(2)], axis=-1) for c in range(bps)]
    probs = []
    for c in range(bps):
        j = jj * bps + c
        gs_ref[j] = jnp.sum(raw[c], axis=-1, keepdims=True) * (1.0 / (MOBA_BLOCK * scale))
        if c == bps - 1:
            s = raw[c] + jnp.where(j == nb - 1, blast_ref[...], cfar_ref[...])
        else:
            s = raw[c] + cfar_ref[...]
        m = jnp.max(s, axis=-1, keepdims=True)
        p = jnp.exp(s - m)
        ms_ref[j] = m
        ls_ref[j] = jnp.sum(p, axis=-1, keepdims=True)
        probs.append(p.astype(BF16))
    for c in range(bps):
        pb = probs[c]
        half = pb.shape[-1] // 2
        r_ref[jj * bps + c] = (
            lax.dot_general(pb[..., :half], v_refs[2 * c][0, 0].astype(BF16), bnt, preferred_element_type=F32)
            + lax.dot_general(pb[..., half:], v_refs[2 * c + 1][0, 0].astype(BF16), bnt,
                              preferred_element_type=F32))

    @pl.when(jj == nb // bps - 1)
    def _():
        knew, vnew = knew_ref[0], vnew_ref[0]
        qf = q3 * scale
        s_o = jnp.concatenate([jnp.sum(qf * knew[:, t:t + 1, :], axis=-1, keepdims=True) for t in range(tdec)],
                              axis=-1) + bown_ref[...]
        m_o = jnp.max(s_o, axis=-1, keepdims=True)
        p_o = jnp.exp(s_o - m_o)
        l_o = jnp.sum(p_o, axis=-1, keepdims=True)
        r_o = p_o[..., 0:1] * vnew[:, 0:1, :]
        for t in range(1, tdec):
            r_o = r_o + p_o[..., t:t + 1] * vnew[:, t:t + 1, :]
        g = gs_ref[...]
        n_iota = lax.broadcasted_iota(jnp.int32, g.shape, 0)
        sel = _top_k_mask(g, n_iota, min(MOBA_TOPK, nb), nb)
        ms = jnp.where(sel, ms_ref[...], NEG)
        mtot = jnp.maximum(jnp.max(ms, axis=0), m_o)
        w = jnp.where(sel, jnp.exp(ms - mtot[None]), 0.0)
        w_o = jnp.exp(m_o - mtot)
        ltot = w_o * l_o + jnp.sum(w * ls_ref[...], axis=0)
        acc = w_o * r_o + jnp.sum(w * r_ref[...], axis=0)
        o_ref[0] = acc / ltot


def _attn_sample(page_table, rel_bias, cache_kt, cache_vt, layer, q3, knew, vnew, bks, bkn):
    dec_b, _, qp, _ = q3.shape
    tdec = knew.shape[2]
    page = cache_kt.shape[4]
    assert 2 * page == MOBA_BLOCK, "one MoBA block is swept as two cache pages"
    nb = page_table.shape[1] // 2
    bps = math.gcd(SAMPLE_BLOCKS_PER_STEP, nb)
    kern = functools.partial(_attn_sample_kernel, nb=nb, bps=bps, tdec=tdec, n_buckets=rel_bias.shape[0])
    pg = lambda o: pl.BlockSpec((1, 1, N_HEADS, HEAD_DIM, page),
                                lambda b, jj, pt: (pt[b, 2 * bps * jj + o], layer, 0, 0, 0))
    full = lambda a: pl.BlockSpec(a.shape, lambda b, jj, pt: (0,) * a.ndim)
    per_b = lambda a: pl.BlockSpec((1,) + a.shape[1:], lambda b, jj, pt: (b,) + (0,) * (a.ndim - 1))
    stat = pltpu.VMEM((nb, N_HEADS, qp, 1), F32)
    grid_spec = pltpu.PrefetchScalarGridSpec(
        num_scalar_prefetch=1, grid=(dec_b, nb // bps),
        in_specs=([pl.BlockSpec(memory_space=pltpu.SMEM)] + [pg(o) for o in range(2 * bps)] * 2
                  + [per_b(q3), per_b(knew), per_b(vnew), full(bks), full(bkn)]),
        out_specs=pl.BlockSpec((1, N_HEADS, qp, HEAD_DIM), lambda b, jj, pt: (b, 0, 0, 0)),
        scratch_shapes=[pltpu.VMEM((N_HEADS,) + bks.shape, F32), pltpu.VMEM((N_HEADS,) + bks.shape, F32),
                        pltpu.VMEM((N_HEADS,) + bkn.shape, F32), stat, stat, stat,
                        pltpu.VMEM((nb, N_HEADS, qp, HEAD_DIM), F32)],
    )
    return pl.pallas_call(
        kern, grid_spec=grid_spec, out_shape=jax.ShapeDtypeStruct((dec_b, N_HEADS, qp, HEAD_DIM), F32),
        compiler_params=pltpu.CompilerParams(dimension_semantics=("arbitrary", "arbitrary"),
                                             vmem_limit_bytes=VMEM_LIMIT),
        name="attn_sample",
    )(page_table, rel_bias, *([cache_kt] * (2 * bps)), *([cache_vt] * (2 * bps)), q3, knew, vnew, bks, bkn)


def _merge_sample_kernel(yb_ref, gbs_ref, hac_ref, smb_ref, x_ref, pb_ref, wout_ref, lng_ref, lnb_ref, o_ref, *, alpha):
    o_ref[...] = _merge(yb_ref[...], gbs_ref[...], hac_ref[...], smb_ref[...], x_ref[...], pb_ref, wout_ref,
                        lng_ref, lnb_ref, alpha)


def _merge_sample(yb, gbs, hac, smb, x, pb, wout, lng, lnb, alpha):
    return pl.pallas_call(
        functools.partial(_merge_sample_kernel, alpha=alpha),
        out_shape=jax.ShapeDtypeStruct(x.shape, F32),
        compiler_params=pltpu.CompilerParams(vmem_limit_bytes=VMEM_LIMIT),
        name="merge_sample",
    )(yb, gbs, hac, smb, x, pb, wout, lng, lnb)


def _block_diag(w):
    g, c, d = w.shape
    out = jnp.zeros((g * c, g * d), w.dtype)
    for i in range(g):
        out = out.at[i * c:(i + 1) * c, i * d:(i + 1) * d].set(w[i])
    return out


def kernel(x_prompt, x_sample, cache_k, cache_v, page_table, state_pool, state_conv, rel_bias, w_in, w_pool, s_pool,
           w_dw, b_dw, conv_ln_g, conv_ln_b, w_pw2, p_a, p_b, p_c, w_out, ln_g, ln_b):
    depth = w_in.shape[0]
    bsz, t, d_model = x_prompt.shape
    dec_b, tdec, _ = x_sample.shape
    d_pool = w_pool.shape[1] * w_pool.shape[2]
    d_attn, d_conv = p_b.shape[1], p_c.shape[1]
    assert d_attn == N_HEADS * HEAD_DIM and t % MOBA_BLOCK == 0
    page = cache_k.shape[2]
    past_len = page_table.shape[1] * page
    assert past_len % MOBA_BLOCK == 0 and past_len >= POOL_HALO
    sp = _splits(d_pool, d_attn, d_conv, d_model)
    alpha = (2 * depth) ** 0.25
    n_buckets = rel_bias.shape[0]

    bucket = _bucket_table(n_buckets, 2 * MOBA_BLOCK + tdec)
    far_rel = int(np.argmax(bucket == n_buckets - 1))
    assert (bucket[far_rel:] == n_buckets - 1).all() and far_rel <= MOBA_BLOCK
    d_own = np.arange(MOBA_BLOCK)[None, :] - np.arange(MOBA_BLOCK)[:, None]
    bko = jnp.asarray(np.where(d_own >= 0, bucket[np.maximum(d_own, 0)], -1).astype(np.int32))
    bkp = jnp.asarray(bucket[d_own + MOBA_BLOCK].astype(np.int32))
    qp = -(-tdec // 8) * 8
    dq = np.minimum(np.arange(qp), tdec - 1)[:, None]
    bks = jnp.asarray(bucket[MOBA_BLOCK + dq - np.arange(MOBA_BLOCK)[None, :]].astype(np.int32))
    d_new = dq - np.arange(tdec)[None, :]
    bkn = jnp.asarray(np.where(d_new >= 0, bucket[np.maximum(d_new, 0)], -1).astype(np.int32))
    rel = rel_bias.astype(F32)
    cache_kt = cache_k.transpose(0, 1, 3, 4, 2)
    cache_vt = cache_v.transpose(0, 1, 3, 4, 2)
    kall = jnp.zeros((bsz, depth, d_attn, t), F32)
    vall = jnp.zeros((bsz, depth, d_attn, t), F32)

    xp = x_prompt
    xs = x_sample.transpose(1, 0, 2).reshape(tdec * dec_b, d_model)
    outs = {n: [] for n in ("ks", "vs", "pp", "ps", "cp", "cs")}
    for l in range(depth):
        win = w_in[l].astype(BF16)
        wpool = _block_diag(w_pool[l]).astype(BF16)
        wl = (win, wpool, s_pool[l][None], w_dw[l], b_dw[l][None], conv_ln_g[l][None], conv_ln_b[l][None],
              w_pw2[l].astype(BF16), p_a[l].astype(BF16), p_c[l].astype(BF16))
        pb, wout, lng, lnb = p_b[l].astype(BF16), w_out[l].astype(BF16), ln_g[l][None], ln_b[l][None]

        kall, vall, kb, vt, qt, kmean, gbs, smb, hac, pst, cst = _proj_prompt(xp, *wl, kall, vall, l, sp)
        xp = _attn_prompt(rel, qt, kb, vt, kmean.reshape(bsz, -1, d_attn), bko, bkp, xp, hac, smb, gbs, pb, wout,
                          lng, lnb, alpha)
        outs["pp"].append(pst)
        outs["cp"].append(cst)

        pst_tm = state_pool[:, l].transpose(1, 0, 2).reshape(-1, d_pool)
        cst_tm = state_conv[:, l].transpose(1, 0, 2).reshape(-1, d_conv)
        q_s, k_s, v_s, gbs_s, smb_s, hac_s, pst_s, cst_s = _proj_sample(xs, pst_tm, cst_tm, *wl, sp, dec_b)
        to_h = lambda a: a.reshape(tdec, dec_b, N_HEADS, HEAD_DIM).transpose(1, 2, 0, 3)
        q_h, k_h, v_h = to_h(q_s), to_h(k_s), to_h(v_s)
        q3 = jnp.pad(q_h, ((0, 0), (0, 0), (0, qp - tdec), (0, 0)))
        yb_s = _attn_sample(page_table, rel, cache_kt, cache_vt, l, q3, k_h, v_h, bks, bkn)
        yb_tm = yb_s[:, :, :tdec].transpose(2, 0, 1, 3).reshape(tdec * dec_b, d_attn)
        k_b, v_b = k_h.transpose(0, 2, 1, 3), v_h.transpose(0, 2, 1, 3)
        xs = _merge_sample(yb_tm, gbs_s, hac_s, smb_s, xs, pb, wout, lng, lnb, alpha)
        outs["ks"].append(k_b)
        outs["vs"].append(v_b)
        outs["ps"].append(pst_s.reshape(-1, dec_b, d_pool).transpose(1, 0, 2))
        outs["cs"].append(cst_s.reshape(-1, dec_b, d_conv).transpose(1, 0, 2))

    st = lambda n: jnp.stack(outs[n], axis=1)
    y_sample = xs.reshape(tdec, dec_b, d_model).transpose(1, 0, 2)
    to_out = lambda a: a.reshape(bsz, depth, N_HEADS, HEAD_DIM, t).transpose(0, 1, 4, 2, 3)
    return (xp, y_sample, to_out(kall), to_out(vall), st("ks"), st("vs"), st("pp"), st("ps"), st("cp"), st("cs"))
```

```python
import functools
import math

import numpy as np
import jax
import jax.numpy as jnp
from jax import lax
from jax.experimental import pallas as pl
from jax.experimental.pallas import tpu as pltpu

POOL_WINDOWS = (2, 4, 8, 16)
POOL_HALO = 16
CONV_W = 31
CONV_HALO = 32
N_HEADS = 8
HEAD_DIM = 64
MOBA_BLOCK = 256
MOBA_TOPK = 3
REL_MAX_DIST = 128
LN_EPS = 1e-5
NEG = -1e30
LOG2E = 1.4426950408889634
LANES = 128
VMEM_LIMIT = 56 * 1024 * 1024
SAMPLE_BLOCKS_PER_STEP = 16
FAR_UNROLL = 4
PROJ_BLOCKS_PER_TILE = 1
ONES_ROWS = 16
VT_ROWS = HEAD_DIM + ONES_ROWS

F32 = jnp.float32
BF16 = jnp.bfloat16


def _sigmoid(x):
    return 1.0 / (1.0 + jnp.exp(-x))


def _silu(x):
    return x * _sigmoid(x)


def _layer_norm(x, g, b):
    mu = jnp.mean(x, axis=-1, keepdims=True)
    xc = x - mu
    var = jnp.mean(xc * xc, axis=-1, keepdims=True)
    return xc * lax.rsqrt(var + LN_EPS) * g + b


def _bucket_table(n_buckets, n):
    rel = np.arange(n)
    max_exact = n_buckets // 2
    nf = np.maximum(rel, 1).astype(np.float32)
    large = max_exact + (np.log(nf / np.float32(max_exact)) / np.float32(math.log(REL_MAX_DIST / max_exact))
                         * np.float32(n_buckets - max_exact)).astype(np.int32)
    large = np.minimum(large, n_buckets - 1)
    return np.where(rel < max_exact, rel, large).astype(np.int32)


def _splits(d_pool, d_attn, d_conv, d_model):
    names = ("u_a", "g_a", "q", "k", "v", "g_b", "c_val", "c_gate", "g_c", "m_a", "m_b", "m_c")
    widths = (d_pool, d_pool, d_attn, d_attn, d_attn, d_attn, d_conv, d_conv, d_conv, d_model, d_model, d_model)
    off, out = 0, {}
    for nme, w in zip(names, widths):
        out[nme] = (off, off + w)
        off += w
    return out


def _top_k_mask(g, n_iota, k, n):
    sel = jnp.zeros(g.shape, jnp.bool_)
    for _ in range(k):
        mx = jnp.max(g, axis=0, keepdims=True)
        idx = jnp.min(jnp.where(g == mx, n_iota, n), axis=0, keepdims=True)
        pick = n_iota == idx
        sel = jnp.logical_or(sel, pick)
        g = jnp.where(pick, -jnp.inf, g)
    return sel


def _branches(seg, extu_ref, exta_ref, cnt, wpool_ref, spool_ref, wdw_ref, bdw_ref, clng_ref, clnb_ref,
              wpw2_ref, pa_ref, pc_ref, *, tm, rs, d_pool, between=None):
    extu_ref[pl.ds(POOL_HALO * rs, tm), :] = seg("u_a")
    exta_ref[pl.ds(CONV_HALO * rs, tm), :] = seg("c_val") * _sigmoid(seg("c_gate"))
    if between is not None:
        between[0]()

    u = extu_ref[pl.ds(POOL_HALO * rs, tm), :]
    gdim = d_pool // len(POOL_WINDOWS)
    lane = lax.broadcasted_iota(jnp.int32, (1, d_pool), 1)
    acc = u
    num = None
    j = 1
    for g, w in enumerate(POOL_WINDOWS):
        while j < w:
            acc = acc + extu_ref[pl.ds((POOL_HALO - j) * rs, tm), :]
            j += 1
        num = acc if num is None else jnp.where(lane >= g * gdim, acc, num)
    pooled = num / cnt
    r = pooled - u
    ya = jnp.dot(r.astype(BF16), wpool_ref[...], preferred_element_type=F32) * spool_ref[...]
    ya = ya * _silu(seg("g_a"))
    hac = _sigmoid(seg("m_a")) * jnp.dot(ya.astype(BF16), pa_ref[...], preferred_element_type=F32)
    if between is not None:
        between[1]()

    h = jnp.zeros((tm, exta_ref.shape[1]), F32) + bdw_ref[...]
    for j in range(CONV_W):
        h = h + wdw_ref[j:j + 1, :] * exta_ref[pl.ds((CONV_HALO - (CONV_W - 1) + j) * rs, tm), :]
    h = _silu(_layer_norm(h, clng_ref[...], clnb_ref[...]))
    yc = jnp.dot(h.astype(BF16), wpw2_ref[...], preferred_element_type=F32) * _silu(seg("g_c"))
    hac = hac + _sigmoid(seg("m_c")) * jnp.dot(yc.astype(BF16), pc_ref[...], preferred_element_type=F32)
    return hac


def _merge(yb, gbs, hac, smb, x, pb_ref, wout_ref, lng_ref, lnb_ref, alpha):
    ybg = (yb * gbs).astype(BF16)
    h = hac + smb * jnp.dot(ybg, pb_ref[...], preferred_element_type=F32)
    y = alpha * x + jnp.dot(h.astype(BF16), wout_ref[...], preferred_element_type=F32)
    return _layer_norm(y, lng_ref[...], lnb_ref[...])


def _proj_prompt_kernel(x_ref, win_ref, wpool_ref, spool_ref, wdw_ref, bdw_ref, clng_ref, clnb_ref, wpw2_ref,
                        pa_ref, pc_ref, kall_in_ref, vall_in_ref,
                        kall_ref, vall_ref, kb_ref, vt_ref, qt_ref, kmean_ref, gbs_ref, smb_ref, hac_ref, pst_ref,
                        cst_ref, extu_ref, exta_ref, *, tm, sp, d_pool, n_tiles):
    del kall_in_ref, vall_in_ref
    i = pl.program_id(1)
    xb = x_ref[0].astype(BF16)

    def seg(name):
        a, b = sp[name]
        return jnp.dot(xb, win_ref[:, a:b], preferred_element_type=F32)

    @pl.when(i == 0)
    def _():
        extu_ref[0:POOL_HALO, :] = jnp.zeros((POOL_HALO, d_pool), F32)
        exta_ref[0:CONV_HALO, :] = jnp.zeros((CONV_HALO, exta_ref.shape[1]), F32)

    pos1 = i * tm + lax.broadcasted_iota(jnp.int32, (tm, 1), 0) + 1
    wl = jnp.zeros((1, d_pool), jnp.int32)
    lane = lax.broadcasted_iota(jnp.int32, (1, d_pool), 1)
    gdim = d_pool // len(POOL_WINDOWS)
    for g, w in enumerate(POOL_WINDOWS):
        wl = jnp.where(lane >= g * gdim, w, wl)
    cnt = jnp.minimum(pos1, wl).astype(F32)

    def attention_inputs():
        qt = seg("q").T
        k = seg("k")
        kall_ref[0, 0] = k.T
        vt = seg("v").T
        vall_ref[0, 0] = vt
        vtb = vt.astype(BF16)
        for j in range(tm // MOBA_BLOCK):
            rws = slice(j * MOBA_BLOCK, (j + 1) * MOBA_BLOCK)
            qt_ref[0, j] = qt[:, rws]
            kb_ref[0, j] = k[rws].astype(BF16)
            kmean_ref[0, j] = jnp.sum(k[rws], axis=0, keepdims=True) * (1.0 / MOBA_BLOCK)
            for h in range(N_HEADS):
                vt_ref[0, j, h * VT_ROWS:h * VT_ROWS + HEAD_DIM, :] = vtb[h * HEAD_DIM:(h + 1) * HEAD_DIM, rws]
                vt_ref[0, j, h * VT_ROWS + HEAD_DIM:(h + 1) * VT_ROWS, :] = jnp.ones((ONES_ROWS, MOBA_BLOCK), BF16)

    def attention_gates():
        gbs_ref[0] = _silu(seg("g_b"))
        smb_ref[0] = _sigmoid(seg("m_b"))

    hac_ref[0] = _branches(seg, extu_ref, exta_ref, cnt, wpool_ref, spool_ref, wdw_ref, bdw_ref, clng_ref,
                           clnb_ref, wpw2_ref, pa_ref, pc_ref, tm=tm, rs=1, d_pool=d_pool,
                           between=(attention_inputs, attention_gates))

    @pl.when(i == n_tiles - 1)
    def _():
        pst_ref[0] = extu_ref[pl.ds(tm + 1, POOL_HALO - 1), :]
        cst_ref[0] = exta_ref[pl.ds(tm + CONV_HALO - (CONV_W - 1), CONV_W - 1), :]

    extu_ref[0:POOL_HALO, :] = extu_ref[pl.ds(tm, POOL_HALO), :]
    exta_ref[0:CONV_HALO, :] = exta_ref[pl.ds(tm, CONV_HALO), :]


def _const_spec(shape):
    nd = len(shape)
    return pl.BlockSpec(shape, lambda *_: (0,) * nd, pipeline_mode=pl.Buffered(1))


def _proj_prompt(x, win, wpool, spool, wdw, bdw, clng, clnb, wpw2, pa, pc, kall, vall, layer, sp):
    bsz, t, d_model = x.shape
    bpt = math.gcd(PROJ_BLOCKS_PER_TILE, t // MOBA_BLOCK)
    tm = bpt * MOBA_BLOCK
    n_tiles, n_blocks = t // tm, t // MOBA_BLOCK
    d_pool, d_conv, d_attn = wpool.shape[0], wpw2.shape[0], sp["q"][1] - sp["q"][0]
    kern = functools.partial(_proj_prompt_kernel, tm=tm, sp=sp, d_pool=d_pool, n_tiles=n_tiles)
    row = lambda w: pl.BlockSpec((1, tm, w), lambda b, i: (b, i, 0))
    blk = lambda r, c: pl.BlockSpec((1, bpt, r, c), lambda b, i: (b, i, 0, 0))
    out_shape = (
        jax.ShapeDtypeStruct(kall.shape, F32),
        jax.ShapeDtypeStruct(vall.shape, F32),
        jax.ShapeDtypeStruct((bsz, n_blocks, MOBA_BLOCK, d_attn), BF16),
        jax.ShapeDtypeStruct((bsz, n_blocks, N_HEADS * VT_ROWS, MOBA_BLOCK), BF16),
        jax.ShapeDtypeStruct((bsz, n_blocks, d_attn, MOBA_BLOCK), F32),
        jax.ShapeDtypeStruct((bsz, n_blocks, 1, d_attn), F32),
        jax.ShapeDtypeStruct((bsz, t, d_attn), F32),
        jax.ShapeDtypeStruct((bsz, t, d_model), F32),
        jax.ShapeDtypeStruct((bsz, t, d_model), F32),
        jax.ShapeDtypeStruct((bsz, POOL_HALO - 1, d_pool), F32),
        jax.ShapeDtypeStruct((bsz, CONV_W - 1, d_conv), F32),
    )
    tall = pl.BlockSpec((1, 1, d_attn, tm), lambda b, i: (b, layer, 0, i))
    out_specs = (
        tall, tall, blk(MOBA_BLOCK, d_attn), blk(N_HEADS * VT_ROWS, MOBA_BLOCK), blk(d_attn, MOBA_BLOCK), blk(1, d_attn),
        row(d_attn), row(d_model), row(d_model),
        pl.BlockSpec((1, POOL_HALO - 1, d_pool), lambda b, i: (b, 0, 0)),
        pl.BlockSpec((1, CONV_W - 1, d_conv), lambda b, i: (b, 0, 0)),
    )
    in_specs = ([row(d_model)] + [_const_spec(a.shape) for a in (win, wpool, spool, wdw, bdw, clng, clnb, wpw2, pa, pc)]
                + [pl.BlockSpec(memory_space=pl.ANY)] * 2)
    return pl.pallas_call(
        kern, grid=(bsz, n_tiles), in_specs=in_specs, out_specs=out_specs, out_shape=out_shape,
        input_output_aliases={11: 0, 12: 1},
        scratch_shapes=[pltpu.VMEM((POOL_HALO + tm, d_pool), F32), pltpu.VMEM((CONV_HALO + tm, d_conv), F32)],
        compiler_params=pltpu.CompilerParams(dimension_semantics=("arbitrary", "arbitrary"),
                                             vmem_limit_bytes=VMEM_LIMIT),
        name="proj_prompt",
    )(x, win, wpool, spool, wdw, bdw, clng, clnb, wpw2, pa, pc, kall, vall)


def _attn_prompt_kernel(rel_ref, qt_ref, kb_ref, vt_ref, kmean_ref, bko_ref, bkp_ref, x_ref, hac_ref, smb_ref,
                        gbs_ref, pb_ref, wout_ref, lng_ref, lnb_ref, o_ref,
                        bown_ref, bprev_ref, msk_ref, qb_ref, s_ref, mb_ref, p_ref, m_ref, l_ref, a_ref, acc_ref, yb_ref,
                        *, nb, tq, alpha, n_buckets):
    i = pl.program_id(1)

    @pl.when(jnp.logical_and(pl.program_id(0) == 0, i == 0))
    def _():
        for h in range(N_HEADS):
            bown_ref[h] = jnp.full((tq, tq), NEG, F32)
            bprev_ref[h] = jnp.zeros((tq, tq), F32)

            def fill(bkt, c):
                val = rel_ref[bkt, h] * LOG2E
                bown_ref[h] = jnp.where(bko_ref[...] == bkt, val, bown_ref[h])
                bprev_ref[h] = jnp.where(bkp_ref[...] == bkt, val, bprev_ref[h])
                return c

            lax.fori_loop(0, n_buckets, fill, 0)

    n_iota = lax.broadcasted_iota(jnp.int32, (nb, tq), 0)
    past = n_iota < i
    half = lax.broadcasted_iota(jnp.int32, (2 * HEAD_DIM, tq), 0) // HEAD_DIM
    pair = lambda h: slice((h // 2) * 2 * HEAD_DIM, (h // 2 + 1) * 2 * HEAD_DIM)
    rows = lambda h: slice(h * HEAD_DIM, (h + 1) * HEAD_DIM)

    for h in range(N_HEADS):
        qh = jnp.where(half == h % 2, qt_ref[0, 0, pair(h), :], 0.0)
        gate = jnp.dot(kmean_ref[0, :, pair(h)].astype(BF16), qh.astype(BF16),
                       preferred_element_type=F32)
        sel = _top_k_mask(jnp.where(past, gate, NEG), n_iota, min(MOBA_TOPK, nb), nb)
        msk_ref[h] = jnp.where(jnp.logical_and(sel, past), 0.0, NEG)
        qb_ref[h] = (qh * (HEAD_DIM ** -0.5 * LOG2E)).astype(BF16)
        m_ref[h] = jnp.full((1, tq), NEG, F32)
        l_ref[h] = jnp.zeros((1, tq), F32)
        acc_ref[h] = jnp.zeros((HEAD_DIM, tq), F32)

    def scores(n, slot, with_max):
        for h in range(N_HEADS):
            s = jnp.dot(kb_ref[0, n, :, pair(h)], qb_ref[h], preferred_element_type=F32)
            s_ref[slot, h] = s
            if with_max:
                mb_ref[slot, h] = jnp.max(s, axis=0, keepdims=True)

    def softmax_pv(n, slot, bias_ref, shift_of):
        for h in range(N_HEADS):
            m_old = m_ref[h]
            if bias_ref is None:
                s = s_ref[slot, h]
                shift = shift_of(h)
                m_new = jnp.maximum(m_old, mb_ref[slot, h] + shift)
                t = s + (shift - m_new)
            else:
                s = s_ref[slot, h] + bias_ref[h]
                shift = shift_of(h)
                m_blk = jnp.max(s, axis=0, keepdims=True)
                m_new = jnp.maximum(m_old, m_blk if shift is None else m_blk + shift)
                t = s - m_new if shift is None else s + (shift - m_new)
            a_ref[h] = jnp.exp2(m_old - m_new)
            m_ref[h] = m_new
            p_ref[h] = jnp.exp2(t.astype(BF16))
        for h in range(N_HEADS):
            pv = jnp.dot(vt_ref[0, n, h * VT_ROWS:(h + 1) * VT_ROWS, :], p_ref[h], preferred_element_type=F32)
            acc_ref[h] = a_ref[h] * acc_ref[h] + pv[:HEAD_DIM]
            l_ref[h] = a_ref[h] * l_ref[h] + pv[HEAD_DIM:HEAD_DIM + 1]

    @pl.when(i == 0)
    def _():
        scores(i, 0, False)
        softmax_pv(i, 0, bown_ref, lambda h: None)

    @pl.when(i >= 1)
    def _():
        scores(i, 0, False)
        scores(i - 1, 1, False)
        softmax_pv(i, 0, bown_ref, lambda h: None)
        softmax_pv(i - 1, 1, bprev_ref, lambda h: msk_ref[h, pl.ds(i - 1, 1), :])

    def far_pass(blocks):
        for u, n in enumerate(blocks):
            scores(n, u, True)
        for u, n in enumerate(blocks):
            softmax_pv(n, u, None, lambda h, n=n: msk_ref[h, pl.ds(n, 1), :] + rel_ref[n_buckets - 1, h] * LOG2E)

    def far_group(k, c):
        far_pass([FAR_UNROLL * k + u for u in range(FAR_UNROLL)])
        return c

    n_far = jnp.maximum(i - 1, 0)
    n_grp = n_far // FAR_UNROLL
    lax.fori_loop(0, n_grp, far_group, 0)
    assert FAR_UNROLL == 4
    rem = n_far - n_grp * FAR_UNROLL

    @pl.when(rem >= 2)
    def _():
        far_pass([n_grp * FAR_UNROLL, n_grp * FAR_UNROLL + 1])

    @pl.when(rem % 2 == 1)
    def _():
        far_pass([n_far - 1])

    for h in range(N_HEADS):
        o = acc_ref[h] / l_ref[h]
        yb_ref[:, rows(h)] = o.T

    o_ref[0] = _merge(yb_ref[...], gbs_ref[0], hac_ref[0], smb_ref[0], x_ref[0], pb_ref, wout_ref, lng_ref,
                      lnb_ref, alpha)


def _attn_prompt(rel_bias, qt, kb, vt, kmean, bko, bkp, x, hac, smb, gbs, pb, wout, lng, lnb, alpha):
    bsz, t, d_model = x.shape
    nb, tq = kb.shape[1], kb.shape[2]
    d_attn = kb.shape[3]
    kern = functools.partial(_attn_prompt_kernel, nb=nb, tq=tq, alpha=alpha, n_buckets=rel_bias.shape[0])
    row = lambda w: pl.BlockSpec((1, tq, w), lambda b, i: (b, i, 0))
    per_b = lambda a: pl.BlockSpec((1,) + a.shape[1:], lambda b, i: (b,) + (0,) * (a.ndim - 1),
                                   pipeline_mode=pl.Buffered(1))
    in_specs = [
        pl.BlockSpec(memory_space=pltpu.SMEM),
        pl.BlockSpec((1, 1, d_attn, tq), lambda b, i: (b, i, 0, 0)),
        per_b(kb), per_b(vt), per_b(kmean),
        _const_spec(bko.shape), _const_spec(bkp.shape),
        row(d_model), row(d_model), row(d_model), row(d_attn),
        _const_spec(pb.shape), _const_spec(wout.shape), _const_spec(lng.shape), _const_spec(lnb.shape),
    ]
    return pl.pallas_call(
        kern, grid=(bsz, nb), in_specs=in_specs, out_specs=row(d_model),
        out_shape=jax.ShapeDtypeStruct((bsz, t, d_model), F32),
        scratch_shapes=[pltpu.VMEM((N_HEADS, tq, tq), F32), pltpu.VMEM((N_HEADS, tq, tq), F32),
                        pltpu.VMEM((N_HEADS, nb, tq), F32), pltpu.VMEM((N_HEADS, 2 * HEAD_DIM, tq), BF16),
                        pltpu.VMEM((FAR_UNROLL, N_HEADS, tq, tq), F32), pltpu.VMEM((FAR_UNROLL, N_HEADS, 1, tq), F32),
                        pltpu.VMEM((N_HEADS, tq, tq), BF16),
                        pltpu.VMEM((N_HEADS, 1, tq), F32), pltpu.VMEM((N_HEADS, 1, tq), F32),
                        pltpu.VMEM((N_HEADS, 1, tq), F32),
                        pltpu.VMEM((N_HEADS, HEAD_DIM, tq), F32), pltpu.VMEM((tq, d_attn), F32)],
        compiler_params=pltpu.CompilerParams(dimension_semantics=("arbitrary", "arbitrary"),
                                             vmem_limit_bytes=VMEM_LIMIT),
        name="attn_prompt",
    )(rel_bias, qt, kb, vt, kmean, bko, bkp, x, hac, smb, gbs, pb, wout, lng, lnb)


def _proj_sample_kernel(x_ref, pst_in_ref, cst_in_ref, win_ref, wpool_ref, spool_ref, wdw_ref, bdw_ref, clng_ref,
                        clnb_ref, wpw2_ref, pa_ref, pc_ref,
                        q_ref, k_ref, v_ref, gbs_ref, smb_ref, hac_ref, pst_ref, cst_ref,
                        extu_ref, exta_ref, *, tm, rs, sp, d_pool):
    xb = x_ref[...].astype(BF16)

    def seg(name):
        a, b = sp[name]
        return jnp.dot(xb, win_ref[:, a:b], preferred_element_type=F32)

    n_pool, n_conv = (POOL_HALO - 1) * rs, (CONV_W - 1) * rs
    extu_ref[0:rs, :] = jnp.zeros((rs, d_pool), F32)
    extu_ref[pl.ds(rs, n_pool), :] = pst_in_ref[...]
    exta_ref[0:(CONV_HALO - CONV_W + 1) * rs, :] = jnp.zeros(((CONV_HALO - CONV_W + 1) * rs, exta_ref.shape[1]), F32)
    exta_ref[pl.ds((CONV_HALO - CONV_W + 1) * rs, n_conv), :] = cst_in_ref[...]

    wl = jnp.zeros((1, d_pool), jnp.int32)
    lane = lax.broadcasted_iota(jnp.int32, (1, d_pool), 1)
    gdim = d_pool // len(POOL_WINDOWS)
    for g, w in enumerate(POOL_WINDOWS):
        wl = jnp.where(lane >= g * gdim, w, wl)
    cnt = wl.astype(F32)

    hac_ref[...] = _branches(seg, extu_ref, exta_ref, cnt, wpool_ref, spool_ref, wdw_ref, bdw_ref, clng_ref,
                             clnb_ref, wpw2_ref, pa_ref, pc_ref, tm=tm, rs=rs, d_pool=d_pool)
    pst_ref[...] = extu_ref[pl.ds(POOL_HALO * rs + tm - n_pool, n_pool), :]
    cst_ref[...] = exta_ref[pl.ds(CONV_HALO * rs + tm - n_conv, n_conv), :]
    q_ref[...] = seg("q")
    k_ref[...] = seg("k")
    v_ref[...] = seg("v")
    gbs_ref[...] = _silu(seg("g_b"))
    smb_ref[...] = _sigmoid(seg("m_b"))


def _proj_sample(x_tm, pst_tm, cst_tm, win, wpool, spool, wdw, bdw, clng, clnb, wpw2, pa, pc, sp, rs):
    tm, d_model = x_tm.shape
    d_pool, d_conv, d_attn = wpool.shape[0], wpw2.shape[0], sp["q"][1] - sp["q"][0]
    kern = functools.partial(_proj_sample_kernel, tm=tm, rs=rs, sp=sp, d_pool=d_pool)
    out_shape = (
        jax.ShapeDtypeStruct((tm, d_attn), F32), jax.ShapeDtypeStruct((tm, d_attn), F32),
        jax.ShapeDtypeStruct((tm, d_attn), F32), jax.ShapeDtypeStruct((tm, d_attn), F32),
        jax.ShapeDtypeStruct((tm, d_model), F32), jax.ShapeDtypeStruct((tm, d_model), F32),
        jax.ShapeDtypeStruct(pst_tm.shape, F32), jax.ShapeDtypeStruct(cst_tm.shape, F32),
    )
    return pl.pallas_call(
        kern, out_shape=out_shape,
        scratch_shapes=[pltpu.VMEM((POOL_HALO * rs + tm, d_pool), F32), pltpu.VMEM((CONV_HALO * rs + tm, d_conv), F32)],
        compiler_params=pltpu.CompilerParams(vmem_limit_bytes=VMEM_LIMIT),
        name="proj_sample",
    )(x_tm, pst_tm, cst_tm, win, wpool, spool, wdw, bdw, clng, clnb, wpw2, pa, pc)


def _attn_sample_kernel(pt_ref, rel_ref, *refs, nb, bps, tdec, n_buckets):
    npg = 2 * bps
    k_refs, v_refs = refs[:npg], refs[npg:2 * npg]
    (q_ref, knew_ref, vnew_ref, bks_ref, bkn_ref, o_ref,
     blast_ref, cfar_ref, bown_ref, ms_ref, ls_ref, gs_ref, r_ref) = refs[2 * npg:]
    jj = pl.program_id(1)
    scale = HEAD_DIM ** -0.5
    bnn = (((2,), (1,)), ((0,), (0,)))
    bnt = (((2,), (2,)), ((0,), (0,)))

    @pl.when(jnp.logical_and(pl.program_id(0) == 0, jj == 0))
    def _():
        for h in range(N_HEADS):
            blast_ref[h] = jnp.zeros(blast_ref.shape[1:], F32)
            bown_ref[h] = jnp.full(bown_ref.shape[1:], NEG, F32)
            cfar_ref[h] = jnp.full(cfar_ref.shape[1:], rel_ref[n_buckets - 1, h], F32)

            def fill(bkt, c):
                val = rel_ref[bkt, h]
                blast_ref[h] = jnp.where(bks_ref[...] == bkt, val, blast_ref[h])
                bown_ref[h] = jnp.where(bkn_ref[...] == bkt, val, bown_ref[h])
                return c

            lax.fori_loop(0, n_buckets, fill, 0)

    q3 = q_ref[0]
    qs = (q3 * scale).astype(BF16)

    raw = [jnp.concatenate(
        [lax.dot_general(qs, k_refs[2 * c + o][0, 0].astype(BF16), bnn, preferred_element_type=F32)
         for o in range(2)], axis=-1) for c in range(bps)]
    probs = []
    for c in range(bps):
        j = jj * bps + c
        gs_ref[j] = jnp.sum(raw[c], axis=-1, keepdims=True) * (1.0 / (MOBA_BLOCK * scale))
        if c == bps - 1:
            s = raw[c] + jnp.where(j == nb - 1, blast_ref[...], cfar_ref[...])
        else:
            s = raw[c] + cfar_ref[...]
        m = jnp.max(s, axis=-1, keepdims=True)
        p = jnp.exp(s - m)
        ms_ref[j] = m
        ls_ref[j] = jnp.sum(p, axis=-1, keepdims=True)
        probs.append(p.astype(BF16))
    for c in range(bps):
        pb = probs[c]
        half = pb.shape[-1] // 2
        r_ref[jj * bps + c] = (
            lax.dot_general(pb[..., :half], v_refs[2 * c][0, 0].astype(BF16), bnt, preferred_element_type=F32)
            + lax.dot_general(pb[..., half:], v_refs[2 * c + 1][0, 0].astype(BF16), bnt,
                              preferred_element_type=F32))

    @pl.when(jj == nb // bps - 1)
    def _():
        knew, vnew = knew_ref[0], vnew_ref[0]
        qf = q3 * scale
        s_o = jnp.concatenate([jnp.sum(qf * knew[:, t:t + 1, :], axis=-1, keepdims=True) for t in range(tdec)],
                              axis=-1) + bown_ref[...]
        m_o = jnp.max(s_o, axis=-1, keepdims=True)
        p_o = jnp.exp(s_o - m_o)
        l_o = jnp.sum(p_o, axis=-1, keepdims=True)
        r_o = p_o[..., 0:1] * vnew[:, 0:1, :]
        for t in range(1, tdec):
            r_o = r_o + p_o[..., t:t + 1] * vnew[:, t:t + 1, :]
        g = gs_ref[...]
        n_iota = lax.broadcasted_iota(jnp.int32, g.shape, 0)
        sel = _top_k_mask(g, n_iota, min(MOBA_TOPK, nb), nb)
        ms = jnp.where(sel, ms_ref[...], NEG)
        mtot = jnp.maximum(jnp.max(ms, axis=0), m_o)
        w = jnp.where(sel, jnp.exp(ms - mtot[None]), 0.0)
        w_o = jnp.exp(m_o - mtot)
        ltot = w_o * l_o + jnp.sum(w * ls_ref[...], axis=0)
        acc = w_o * r_o + jnp.sum(w * r_ref[...], axis=0)
        o_ref[0] = acc / ltot


def _attn_sample(page_table, rel_bias, cache_kt, cache_vt, layer, q3, knew, vnew, bks, bkn):
    dec_b, _, qp, _ = q3.shape
    tdec = knew.shape[2]
    page = cache_kt.shape[4]
    assert 2 * page == MOBA_BLOCK, "one MoBA block is swept as two cache pages"
    nb = page_table.shape[1] // 2
    bps = math.gcd(SAMPLE_BLOCKS_PER_STEP, nb)
    kern = functools.partial(_attn_sample_kernel, nb=nb, bps=bps, tdec=tdec, n_buckets=rel_bias.shape[0])
    pg = lambda o: pl.BlockSpec((1, 1, N_HEADS, HEAD_DIM, page),
                                lambda b, jj, pt: (pt[b, 2 * bps * jj + o], layer, 0, 0, 0))
    full = lambda a: pl.BlockSpec(a.shape, lambda b, jj, pt: (0,) * a.ndim)
    per_b = lambda a: pl.BlockSpec((1,) + a.shape[1:], lambda b, jj, pt: (b,) + (0,) * (a.ndim - 1))
    stat = pltpu.VMEM((nb, N_HEADS, qp, 1), F32)
    grid_spec = pltpu.PrefetchScalarGridSpec(
        num_scalar_prefetch=1, grid=(dec_b, nb // bps),
        in_specs=([pl.BlockSpec(memory_space=pltpu.SMEM)] + [pg(o) for o in range(2 * bps)] * 2
                  + [per_b(q3), per_b(knew), per_b(vnew), full(bks), full(bkn)]),
        out_specs=pl.BlockSpec((1, N_HEADS, qp, HEAD_DIM), lambda b, jj, pt: (b, 0, 0, 0)),
        scratch_shapes=[pltpu.VMEM((N_HEADS,) + bks.shape, F32), pltpu.VMEM((N_HEADS,) + bks.shape, F32),
                        pltpu.VMEM((N_HEADS,) + bkn.shape, F32), stat, stat, stat,
                        pltpu.VMEM((nb, N_HEADS, qp, HEAD_DIM), F32)],
    )
    return pl.pallas_call(
        kern, grid_spec=grid_spec, out_shape=jax.ShapeDtypeStruct((dec_b, N_HEADS, qp, HEAD_DIM), F32),
        compiler_params=pltpu.CompilerParams(dimension_semantics=("arbitrary", "arbitrary"),
                                             vmem_limit_bytes=VMEM_LIMIT),
        name="attn_sample",
    )(page_table, rel_bias, *([cache_kt] * (2 * bps)), *([cache_vt] * (2 * bps)), q3, knew, vnew, bks, bkn)


def _merge_sample_kernel(yb_ref, gbs_ref, hac_ref, smb_ref, x_ref, pb_ref, wout_ref, lng_ref, lnb_ref, o_ref, *, alpha):
    o_ref[...] = _merge(yb_ref[...], gbs_ref[...], hac_ref[...], smb_ref[...], x_ref[...], pb_ref, wout_ref,
                        lng_ref, lnb_ref, alpha)


def _merge_sample(yb, gbs, hac, smb, x, pb, wout, lng, lnb, alpha):
    return pl.pallas_call(
        functools.partial(_merge_sample_kernel, alpha=alpha),
        out_shape=jax.ShapeDtypeStruct(x.shape, F32),
        compiler_params=pltpu.CompilerParams(vmem_limit_bytes=VMEM_LIMIT),
        name="merge_sample",
    )(yb, gbs, hac, smb, x, pb, wout, lng, lnb)


def _block_diag(w):
    g, c, d = w.shape
    out = jnp.zeros((g * c, g * d), w.dtype)
    for i in range(g):
        out = out.at[i * c:(i + 1) * c, i * d:(i + 1) * d].set(w[i])
    return out


def kernel(x_prompt, x_sample, cache_k, cache_v, page_table, state_pool, state_conv, rel_bias, w_in, w_pool, s_pool,
           w_dw, b_dw, conv_ln_g, conv_ln_b, w_pw2, p_a, p_b, p_c, w_out, ln_g, ln_b):
    depth = w_in.shape[0]
    bsz, t, d_model = x_prompt.shape
    dec_b, tdec, _ = x_sample.shape
    d_pool = w_pool.shape[1] * w_pool.shape[2]
    d_attn, d_conv = p_b.shape[1], p_c.shape[1]
    assert d_attn == N_HEADS * HEAD_DIM and t % MOBA_BLOCK == 0
    page = cache_k.shape[2]
    past_len = page_table.shape[1] * page
    assert past_len % MOBA_BLOCK == 0 and past_len >= POOL_HALO
    sp = _splits(d_pool, d_attn, d_conv, d_model)
    alpha = (2 * depth) ** 0.25
    n_buckets = rel_bias.shape[0]

    bucket = _bucket_table(n_buckets, 2 * MOBA_BLOCK + tdec)
    far_rel = int(np.argmax(bucket == n_buckets - 1))
    assert (bucket[far_rel:] == n_buckets - 1).all() and far_rel <= MOBA_BLOCK
    d_own = np.arange(MOBA_BLOCK)[None, :] - np.arange(MOBA_BLOCK)[:, None]
    bko = jnp.asarray(np.where(d_own >= 0, bucket[np.maximum(d_own, 0)], -1).astype(np.int32))
    bkp = jnp.asarray(bucket[d_own + MOBA_BLOCK].astype(np.int32))
    qp = -(-tdec // 8) * 8
    dq = np.minimum(np.arange(qp), tdec - 1)[:, None]
    bks = jnp.asarray(bucket[MOBA_BLOCK + dq - np.arange(MOBA_BLOCK)[None, :]].astype(np.int32))
    d_new = dq - np.arange(tdec)[None, :]
    bkn = jnp.asarray(np.where(d_new >= 0, bucket[np.maximum(d_new, 0)], -1).astype(np.int32))
    rel = rel_bias.astype(F32)
    cache_kt = cache_k.transpose(0, 1, 3, 4, 2)
    cache_vt = cache_v.transpose(0, 1, 3, 4, 2)
    kall = jnp.zeros((bsz, depth, d_attn, t), F32)
    vall = jnp.zeros((bsz, depth, d_attn, t), F32)

    xp = x_prompt
    xs = x_sample.transpose(1, 0, 2).reshape(tdec * dec_b, d_model)
    outs = {n: [] for n in ("ks", "vs", "pp", "ps", "cp", "cs")}
    for l in range(depth):
        win = w_in[l].astype(BF16)
        wpool = _block_diag(w_pool[l]).astype(BF16)
        wl = (win, wpool, s_pool[l][None], w_dw[l], b_dw[l][None], conv_ln_g[l][None], conv_ln_b[l][None],
              w_pw2[l].astype(BF16), p_a[l].astype(BF16), p_c[l].astype(BF16))
        pb, wout, lng, lnb = p_b[l].astype(BF16), w_out[l].astype(BF16), ln_g[l][None], ln_b[l][None]

        kall, vall, kb, vt, qt, kmean, gbs, smb, hac, pst, cst = _proj_prompt(xp, *wl, kall, vall, l, sp)
        xp = _attn_prompt(rel, qt, kb, vt, kmean.reshape(bsz, -1, d_attn), bko, bkp, xp, hac, smb, gbs, pb, wout,
                          lng, lnb, alpha)
        outs["pp"].append(pst)
        outs["cp"].append(cst)

        pst_tm = state_pool[:, l].transpose(1, 0, 2).reshape(-1, d_pool)
        cst_tm = state_conv[:, l].transpose(1, 0, 2).reshape(-1, d_conv)
        q_s, k_s, v_s, gbs_s, smb_s, hac_s, pst_s, cst_s = _proj_sample(xs, pst_tm, cst_tm, *wl, sp, dec_b)
        to_h = lambda a: a.reshape(tdec, dec_b, N_HEADS, HEAD_DIM).transpose(1, 2, 0, 3)
        q_h, k_h, v_h = to_h(q_s), to_h(k_s), to_h(v_s)
        q3 = jnp.pad(q_h, ((0, 0), (0, 0), (0, qp - tdec), (0, 0)))
        yb_s = _attn_sample(page_table, rel, cache_kt, cache_vt, l, q3, k_h, v_h, bks, bkn)
        yb_tm = yb_s[:, :, :tdec].transpose(2, 0, 1, 3).reshape(tdec * dec_b, d_attn)
        k_b, v_b = k_h.transpose(0, 2, 1, 3), v_h.transpose(0, 2, 1, 3)
        xs = _merge_sample(yb_tm, gbs_s, hac_s, smb_s, xs, pb, wout, lng, lnb, alpha)
        outs["ks"].append(k_b)
        outs["vs"].append(v_b)
        outs["ps"].append(pst_s.reshape(-1, dec_b, d_pool).transpose(1, 0, 2))
        outs["cs"].append(cst_s.reshape(-1, dec_b, d_conv).transpose(1, 0, 2))

    st = lambda n: jnp.stack(outs[n], axis=1)
    y_sample = xs.reshape(tdec, dec_b, d_model).transpose(1, 0, 2)
    to_out = lambda a: a.reshape(bsz, depth, N_HEADS, HEAD_DIM, t).transpose(0, 1, 4, 2, 3)
    return (xp, y_sample, to_out(kall), to_out(vall), st("ks"), st("vs"), st("pp"), st("ps"), st("cp"), st("cs"))
```

```python
import functools
import math

import numpy as np
import jax
import jax.numpy as jnp
from jax import lax
from jax.experimental import pallas as pl
from jax.experimental.pallas import tpu as pltpu

POOL_WINDOWS = (2, 4, 8, 16)
POOL_HALO = 16
CONV_W = 31
CONV_HALO = 32
N_HEADS = 8
HEAD_DIM = 64
MOBA_BLOCK = 256
MOBA_TOPK = 3
REL_MAX_DIST = 128
LN_EPS = 1e-5
NEG = -1e30
LOG2E = 1.4426950408889634
LANES = 128
VMEM_LIMIT = 56 * 1024 * 1024
SAMPLE_BLOCKS_PER_STEP = 16
FAR_UNROLL = 4
PROJ_BLOCKS_PER_TILE = 1
ONES_ROWS = 16
VT_ROWS = HEAD_DIM + ONES_ROWS

F32 = jnp.float32
BF16 = jnp.bfloat16


def _sigmoid(x):
    return 1.0 / (1.0 + jnp.exp(-x))


def _silu(x):
    return x * _sigmoid(x)


def _layer_norm(x, g, b):
    mu = jnp.mean(x, axis=-1, keepdims=True)
    xc = x - mu
    var = jnp.mean(xc * xc, axis=-1, keepdims=True)
    return xc * lax.rsqrt(var + LN_EPS) * g + b


def _bucket_table(n_buckets, n):
    rel = np.arange(n)
    max_exact = n_buckets // 2
    nf = np.maximum(rel, 1).astype(np.float32)
    large = max_exact + (np.log(nf / np.float32(max_exact)) / np.float32(math.log(REL_MAX_DIST / max_exact))
                         * np.float32(n_buckets - max_exact)).astype(np.int32)
    large = np.minimum(large, n_buckets - 1)
    return np.where(rel < max_exact, rel, large).astype(np.int32)


def _splits(d_pool, d_attn, d_conv, d_model):
    names = ("u_a", "g_a", "q", "k", "v", "g_b", "c_val", "c_gate", "g_c", "m_a", "m_b", "m_c")
    widths = (d_pool, d_pool, d_attn, d_attn, d_attn, d_attn, d_conv, d_conv, d_conv, d_model, d_model, d_model)
    off, out = 0, {}
    for nme, w in zip(names, widths):
        out[nme] = (off, off + w)
        off += w
    return out


def _top_k_mask(g, n_iota, k, n):
    sel = jnp.zeros(g.shape, jnp.bool_)
    for _ in range(k):
        mx = jnp.max(g, axis=0, keepdims=True)
        idx = jnp.min(jnp.where(g == mx, n_iota, n), axis=0, keepdims=True)
        pick = n_iota == idx
        sel = jnp.logical_or(sel, pick)
        g = jnp.where(pick, -jnp.inf, g)
    return sel


def _branches(seg, extu_ref, exta_ref, cnt, wpool_ref, spool_ref, wdw_ref, bdw_ref, clng_ref, clnb_ref,
              wpw2_ref, pa_ref, pc_ref, *, tm, rs, d_pool, between=None):
    extu_ref[pl.ds(POOL_HALO * rs, tm), :] = seg("u_a")
    exta_ref[pl.ds(CONV_HALO * rs, tm), :] = seg("c_val") * _sigmoid(seg("c_gate"))
    if between is not None:
        between[0]()

    u = extu_ref[pl.ds(POOL_HALO * rs, tm), :]
    gdim = d_pool // len(POOL_WINDOWS)
    lane = lax.broadcasted_iota(jnp.int32, (1, d_pool), 1)
    acc = u
    num = None
    j = 1
    for g, w in enumerate(POOL_WINDOWS):
        while j < w:
            acc = acc + extu_ref[pl.ds((POOL_HALO - j) * rs, tm), :]
            j += 1
        num = acc if num is None else jnp.where(lane >= g * gdim, acc, num)
    pooled = num / cnt
    r = pooled - u
    ya = jnp.dot(r.astype(BF16), wpool_ref[...], preferred_element_type=F32) * spool_ref[...]
    ya = ya * _silu(seg("g_a"))
    hac = _sigmoid(seg("m_a")) * jnp.dot(ya.astype(BF16), pa_ref[...], preferred_element_type=F32)
    if between is not None:
        between[1]()

    h = jnp.zeros((tm, exta_ref.shape[1]), F32) + bdw_ref[...]
    for j in range(CONV_W):
        h = h + wdw_ref[j:j + 1, :] * exta_ref[pl.ds((CONV_HALO - (CONV_W - 1) + j) * rs, tm), :]
    h = _silu(_layer_norm(h, clng_ref[...], clnb_ref[...]))
    yc = jnp.dot(h.astype(BF16), wpw2_ref[...], preferred_element_type=F32) * _silu(seg("g_c"))
    hac = hac + _sigmoid(seg("m_c")) * jnp.dot(yc.astype(BF16), pc_ref[...], preferred_element_type=F32)
    return hac


def _merge(yb, gbs, hac, smb, x, pb_ref, wout_ref, lng_ref, lnb_ref, alpha):
    ybg = (yb * gbs).astype(BF16)
    h = hac + smb * jnp.dot(ybg, pb_ref[...], preferred_element_type=F32)
    y = alpha * x + jnp.dot(h.astype(BF16), wout_ref[...], preferred_element_type=F32)
    return _layer_norm(y, lng_ref[...], lnb_ref[...])


def _proj_prompt_kernel(x_ref, win_ref, wpool_ref, spool_ref, wdw_ref, bdw_ref, clng_ref, clnb_ref, wpw2_ref,
                        pa_ref, pc_ref, kall_in_ref, vall_in_ref,
                        kall_ref, vall_ref, kb_ref, vt_ref, qt_ref, kmean_ref, gbs_ref, smb_ref, hac_ref, pst_ref,
                        cst_ref, extu_ref, exta_ref, *, tm, sp, d_pool, n_tiles):
    del kall_in_ref, vall_in_ref
    i = pl.program_id(1)
    xb = x_ref[0].astype(BF16)

    def seg(name):
        a, b = sp[name]
        return jnp.dot(xb, win_ref[:, a:b], preferred_element_type=F32)

    @pl.when(i == 0)
    def _():
        extu_ref[0:POOL_HALO, :] = jnp.zeros((POOL_HALO, d_pool), F32)
        exta_ref[0:CONV_HALO, :] = jnp.zeros((CONV_HALO, exta_ref.shape[1]), F32)

    pos1 = i * tm + lax.broadcasted_iota(jnp.int32, (tm, 1), 0) + 1
    wl = jnp.zeros((1, d_pool), jnp.int32)
    lane = lax.broadcasted_iota(jnp.int32, (1, d_pool), 1)
    gdim = d_pool // len(POOL_WINDOWS)
    for g, w in enumerate(POOL_WINDOWS):
        wl = jnp.where(lane >= g * gdim, w, wl)
    cnt = jnp.minimum(pos1, wl).astype(F32)

    def attention_inputs():
        qt = seg("q").T
        k = seg("k")
        kall_ref[0, 0] = k.T
        vt = seg("v").T
        vall_ref[0, 0] = vt
        vtb = vt.astype(BF16)
        for j in range(tm // MOBA_BLOCK):
            rws = slice(j * MOBA_BLOCK, (j + 1) * MOBA_BLOCK)
            qt_ref[0, j] = qt[:, rws]
            kb_ref[0, j] = k[rws].astype(BF16)
            kmean_ref[0, j] = jnp.sum(k[rws], axis=0, keepdims=True) * (1.0 / MOBA_BLOCK)
            for h in range(N_HEADS):
                vt_ref[0, j, h * VT_ROWS:h * VT_ROWS + HEAD_DIM, :] = vtb[h * HEAD_DIM:(h + 1) * HEAD_DIM, rws]
                vt_ref[0, j, h * VT_ROWS + HEAD_DIM:(h + 1) * VT_ROWS, :] = jnp.ones((ONES_ROWS, MOBA_BLOCK), BF16)

    def attention_gates():
        gbs_ref[0] = _silu(seg("g_b"))
        smb_ref[0] = _sigmoid(seg("m_b"))

    hac_ref[0] = _branches(seg, extu_ref, exta_ref, cnt, wpool_ref, spool_ref, wdw_ref, bdw_ref, clng_ref,
                           clnb_ref, wpw2_ref, pa_ref, pc_ref, tm=tm, rs=1, d_pool=d_pool,
                           between=(attention_inputs, attention_gates))

    @pl.when(i == n_tiles - 1)
    def _():
        pst_ref[0] = extu_ref[pl.ds(tm + 1, POOL_HALO - 1), :]
        cst_ref[0] = exta_ref[pl.ds(tm + CONV_HALO - (CONV_W - 1), CONV_W - 1), :]

    extu_ref[0:POOL_HALO, :] = extu_ref[pl.ds(tm, POOL_HALO), :]
    exta_ref[0:CONV_HALO, :] = exta_ref[pl.ds(tm, CONV_HALO), :]


def _const_spec(shape):
    nd = len(shape)
    return pl.BlockSpec(shape, lambda *_: (0,) * nd, pipeline_mode=pl.Buffered(1))


def _proj_prompt(x, win, wpool, spool, wdw, bdw, clng, clnb, wpw2, pa, pc, kall, vall, layer, sp):
    bsz, t, d_model = x.shape
    bpt = math.gcd(PROJ_BLOCKS_PER_TILE, t // MOBA_BLOCK)
    tm = bpt * MOBA_BLOCK
    n_tiles, n_blocks = t // tm, t // MOBA_BLOCK
    d_pool, d_conv, d_attn = wpool.shape[0], wpw2.shape[0], sp["q"][1] - sp["q"][0]
    kern = functools.partial(_proj_prompt_kernel, tm=tm, sp=sp, d_pool=d_pool, n_tiles=n_tiles)
    row = lambda w: pl.BlockSpec((1, tm, w), lambda b, i: (b, i, 0))
    blk = lambda r, c: pl.BlockSpec((1, bpt, r, c), lambda b, i: (b, i, 0, 0))
    out_shape = (
        jax.ShapeDtypeStruct(kall.shape, F32),
        jax.ShapeDtypeStruct(vall.shape, F32),
        jax.ShapeDtypeStruct((bsz, n_blocks, MOBA_BLOCK, d_attn), BF16),
        jax.ShapeDtypeStruct((bsz, n_blocks, N_HEADS * VT_ROWS, MOBA_BLOCK), BF16),
        jax.ShapeDtypeStruct((bsz, n_blocks, d_attn, MOBA_BLOCK), F32),
        jax.ShapeDtypeStruct((bsz, n_blocks, 1, d_attn), F32),
        jax.ShapeDtypeStruct((bsz, t, d_attn), F32),
        jax.ShapeDtypeStruct((bsz, t, d_model), F32),
        jax.ShapeDtypeStruct((bsz, t, d_model), F32),
        jax.ShapeDtypeStruct((bsz, POOL_HALO - 1, d_pool), F32),
        jax.ShapeDtypeStruct((bsz, CONV_W - 1, d_conv), F32),
    )
    tall = pl.BlockSpec((1, 1, d_attn, tm), lambda b, i: (b, layer, 0, i))
    out_specs = (
        tall, tall, blk(MOBA_BLOCK, d_attn), blk(N_HEADS * VT_ROWS, MOBA_BLOCK), blk(d_attn, MOBA_BLOCK), blk(1, d_attn),
        row(d_attn), row(d_model), row(d_model),
        pl.BlockSpec((1, POOL_HALO - 1, d_pool), lambda b, i: (b, 0, 0)),
        pl.BlockSpec((1, CONV_W - 1, d_conv), lambda b, i: (b, 0, 0)),
    )
    in_specs = ([row(d_model)] + [_const_spec(a.shape) for a in (win, wpool, spool, wdw, bdw, clng, clnb, wpw2, pa, pc)]
                + [pl.BlockSpec(memory_space=pl.ANY)] * 2)
    return pl.pallas_call(
        kern, grid=(bsz, n_tiles), in_specs=in_specs, out_specs=out_specs, out_shape=out_shape,
        input_output_aliases={11: 0, 12: 1},
        scratch_shapes=[pltpu.VMEM((POOL_HALO + tm, d_pool), F32), pltpu.VMEM((CONV_HALO + tm, d_conv), F32)],
        compiler_params=pltpu.CompilerParams(dimension_semantics=("arbitrary", "arbitrary"),
                                             vmem_limit_bytes=VMEM_LIMIT),
        name="proj_prompt",
    )(x, win, wpool, spool, wdw, bdw, clng, clnb, wpw2, pa, pc, kall, vall)


def _attn_prompt_kernel(rel_ref, qt_ref, kb_ref, vt_ref, kmean_ref, bko_ref, bkp_ref, x_ref, hac_ref, smb_ref,
                        gbs_ref, pb_ref, wout_ref, lng_ref, lnb_ref, o_ref,
                        bown_ref, bprev_ref, msk_ref, qb_ref, s_ref, mb_ref, p_ref, m_ref, l_ref, a_ref, acc_ref, yb_ref,
                        *, nb, tq, alpha, n_buckets):
    i = pl.program_id(1)

    @pl.when(jnp.logical_and(pl.program_id(0) == 0, i == 0))
    def _():
        for h in range(N_HEADS):
            bown_ref[h] = jnp.full((tq, tq), NEG, F32)
            bprev_ref[h] = jnp.zeros((tq, tq), F32)

            def fill(bkt, c):
                val = rel_ref[bkt, h] * LOG2E
                bown_ref[h] = jnp.where(bko_ref[...] == bkt, val, bown_ref[h])
                bprev_ref[h] = jnp.where(bkp_ref[...] == bkt, val, bprev_ref[h])
                return c

            lax.fori_loop(0, n_buckets, fill, 0)

    n_iota = lax.broadcasted_iota(jnp.int32, (nb, tq), 0)
    past = n_iota < i
    half = lax.broadcasted_iota(jnp.int32, (2 * HEAD_DIM, tq), 0) // HEAD_DIM
    pair = lambda h: slice((h // 2) * 2 * HEAD_DIM, (h // 2 + 1) * 2 * HEAD_DIM)
    rows = lambda h: slice(h * HEAD_DIM, (h + 1) * HEAD_DIM)

    for h in range(N_HEADS):
        qh = jnp.where(half == h % 2, qt_ref[0, 0, pair(h), :], 0.0)
        gate = jnp.dot(kmean_ref[0, :, pair(h)].astype(BF16), qh.astype(BF16),
                       preferred_element_type=F32)
        sel = _top_k_mask(jnp.where(past, gate, NEG), n_iota, min(MOBA_TOPK, nb), nb)
        msk_ref[h] = jnp.where(jnp.logical_and(sel, past), 0.0, NEG)
        qb_ref[h] = (qh * (HEAD_DIM ** -0.5 * LOG2E)).astype(BF16)
        m_ref[h] = jnp.full((1, tq), NEG, F32)
        l_ref[h] = jnp.zeros((1, tq), F32)
        acc_ref[h] = jnp.zeros((HEAD_DIM, tq), F32)

    def scores(n, slot, with_max):
        for h in range(N_HEADS):
            s = jnp.dot(kb_ref[0, n, :, pair(h)], qb_ref[h], preferred_element_type=F32)
            s_ref[slot, h] = s
            if with_max:
                mb_ref[slot, h] = jnp.max(s, axis=0, keepdims=True)

    def softmax_pv(n, slot, bias_ref, shift_of):
        for h in range(N_HEADS):
            m_old = m_ref[h]
            if bias_ref is None:
                s = s_ref[slot, h]
                shift = shift_of(h)
                m_new = jnp.maximum(m_old, mb_ref[slot, h] + shift)
                t = s + (shift - m_new)
            else:
                s = s_ref[slot, h] + bias_ref[h]
                shift = shift_of(h)
                m_blk = jnp.max(s, axis=0, keepdims=True)
                m_new = jnp.maximum(m_old, m_blk if shift is None else m_blk + shift)
                t = s - m_new if shift is None else s + (shift - m_new)
            a_ref[h] = jnp.exp2(m_old - m_new)
            m_ref[h] = m_new
            p_ref[h] = jnp.exp2(t.astype(BF16))
        for h in range(N_HEADS):
            pv = jnp.dot(vt_ref[0, n, h * VT_ROWS:(h + 1) * VT_ROWS, :], p_ref[h], preferred_element_type=F32)
            acc_ref[h] = a_ref[h] * acc_ref[h] + pv[:HEAD_DIM]
            l_ref[h] = a_ref[h] * l_ref[h] + pv[HEAD_DIM:HEAD_DIM + 1]

    @pl.when(i == 0)
    def _():
        scores(i, 0, False)
        softmax_pv(i, 0, bown_ref, lambda h: None)

    @pl.when(i >= 1)
    def _():
        scores(i, 0, False)
        scores(i - 1, 1, False)
        softmax_pv(i, 0, bown_ref, lambda h: None)
        softmax_pv(i - 1, 1, bprev_ref, lambda h: msk_ref[h, pl.ds(i - 1, 1), :])

    def far_pass(blocks):
        for u, n in enumerate(blocks):
            scores(n, u, True)
        for u, n in enumerate(blocks):
            softmax_pv(n, u, None, lambda h, n=n: msk_ref[h, pl.ds(n, 1), :] + rel_ref[n_buckets - 1, h] * LOG2E)

    def far_group(k, c):
        far_pass([FAR_UNROLL * k + u for u in range(FAR_UNROLL)])
        return c

    n_far = jnp.maximum(i - 1, 0)
    n_grp = n_far // FAR_UNROLL
    lax.fori_loop(0, n_grp, far_group, 0)
    assert FAR_UNROLL == 4
    rem = n_far - n_grp * FAR_UNROLL

    @pl.when(rem >= 2)
    def _():
        far_pass([n_grp * FAR_UNROLL, n_grp * FAR_UNROLL + 1])

    @pl.when(rem % 2 == 1)
    def _():
        far_pass([n_far - 1])

    for h in range(N_HEADS):
        o = acc_ref[h] / l_ref[h]
        yb_ref[:, rows(h)] = o.T

    o_ref[0] = _merge(yb_ref[...], gbs_ref[0], hac_ref[0], smb_ref[0], x_ref[0], pb_ref, wout_ref, lng_ref,
                      lnb_ref, alpha)


def _attn_prompt(rel_bias, qt, kb, vt, kmean, bko, bkp, x, hac, smb, gbs, pb, wout, lng, lnb, alpha):
    bsz, t, d_model = x.shape
    nb, tq = kb.shape[1], kb.shape[2]
    d_attn = kb.shape[3]
    kern = functools.partial(_attn_prompt_kernel, nb=nb, tq=tq, alpha=alpha, n_buckets=rel_bias.shape[0])
    row = lambda w: pl.BlockSpec((1, tq, w), lambda b, i: (b, i, 0))
    per_b = lambda a: pl.BlockSpec((1,) + a.shape[1:], lambda b, i: (b,) + (0,) * (a.ndim - 1),
                                   pipeline_mode=pl.Buffered(1))
    in_specs = [
        pl.BlockSpec(memory_space=pltpu.SMEM),
        pl.BlockSpec((1, 1, d_attn, tq), lambda b, i: (b, i, 0, 0)),
        per_b(kb), per_b(vt), per_b(kmean),
        _const_spec(bko.shape), _const_spec(bkp.shape),
        row(d_model), row(d_model), row(d_model), row(d_attn),
        _const_spec(pb.shape), _const_spec(wout.shape), _const_spec(lng.shape), _const_spec(lnb.shape),
    ]
    return pl.pallas_call(
        kern, grid=(bsz, nb), in_specs=in_specs, out_specs=row(d_model),
        out_shape=jax.ShapeDtypeStruct((bsz, t, d_model), F32),
        scratch_shapes=[pltpu.VMEM((N_HEADS, tq, tq), F32), pltpu.VMEM((N_HEADS, tq, tq), F32),
                        pltpu.VMEM((N_HEADS, nb, tq), F32), pltpu.VMEM((N_HEADS, 2 * HEAD_DIM, tq), BF16),
                        pltpu.VMEM((FAR_UNROLL, N_HEADS, tq, tq), F32), pltpu.VMEM((FAR_UNROLL, N_HEADS, 1, tq), F32),
                        pltpu.VMEM((N_HEADS, tq, tq), BF16),
                        pltpu.VMEM((N_HEADS, 1, tq), F32), pltpu.VMEM((N_HEADS, 1, tq), F32),
                        pltpu.VMEM((N_HEADS, 1, tq), F32),
                        pltpu.VMEM((N_HEADS, HEAD_DIM, tq), F32), pltpu.VMEM((tq, d_attn), F32)],
        compiler_params=pltpu.CompilerParams(dimension_semantics=("arbitrary", "arbitrary"),
                                             vmem_limit_bytes=VMEM_LIMIT),
        name="attn_prompt",
    )(rel_bias, qt, kb, vt, kmean, bko, bkp, x, hac, smb, gbs, pb, wout, lng, lnb)


def _proj_sample_kernel(x_ref, pst_in_ref, cst_in_ref, win_ref, wpool_ref, spool_ref, wdw_ref, bdw_ref, clng_ref,
                        clnb_ref, wpw2_ref, pa_ref, pc_ref,
                        q_ref, k_ref, v_ref, gbs_ref, smb_ref, hac_ref, pst_ref, cst_ref,
                        extu_ref, exta_ref, *, tm, rs, sp, d_pool):
    xb = x_ref[...].astype(BF16)

    def seg(name):
        a, b = sp[name]
        return jnp.dot(xb, win_ref[:, a:b], preferred_element_type=F32)

    n_pool, n_conv = (POOL_HALO - 1) * rs, (CONV_W - 1) * rs
    extu_ref[0:rs, :] = jnp.zeros((rs, d_pool), F32)
    extu_ref[pl.ds(rs, n_pool), :] = pst_in_ref[...]
    exta_ref[0:(CONV_HALO - CONV_W + 1) * rs, :] = jnp.zeros(((CONV_HALO - CONV_W + 1) * rs, exta_ref.shape[1]), F32)
    exta_ref[pl.ds((CONV_HALO - CONV_W + 1) * rs, n_conv), :] = cst_in_ref[...]

    wl = jnp.zeros((1, d_pool), jnp.int32)
    lane = lax.broadcasted_iota(jnp.int32, (1, d_pool), 1)
    gdim = d_pool // len(POOL_WINDOWS)
    for g, w in enumerate(POOL_WINDOWS):
        wl = jnp.where(lane >= g * gdim, w, wl)
    cnt = wl.astype(F32)

    hac_ref[...] = _branches(seg, extu_ref, exta_ref, cnt, wpool_ref, spool_ref, wdw_ref, bdw_ref, clng_ref,
                             clnb_ref, wpw2_ref, pa_ref, pc_ref, tm=tm, rs=rs, d_pool=d_pool)
    pst_ref[...] = extu_ref[pl.ds(POOL_HALO * rs + tm - n_pool, n_pool), :]
    cst_ref[...] = exta_ref[pl.ds(CONV_HALO * rs + tm - n_conv, n_conv), :]
    q_ref[...] = seg("q")
    k_ref[...] = seg("k")
    v_ref[...] = seg("v")
    gbs_ref[...] = _silu(seg("g_b"))
    smb_ref[...] = _sigmoid(seg("m_b"))


def _proj_sample(x_tm, pst_tm, cst_tm, win, wpool, spool, wdw, bdw, clng, clnb, wpw2, pa, pc, sp, rs):
    tm, d_model = x_tm.shape
    d_pool, d_conv, d_attn = wpool.shape[0], wpw2.shape[0], sp["q"][1] - sp["q"][0]
    kern = functools.partial(_proj_sample_kernel, tm=tm, rs=rs, sp=sp, d_pool=d_pool)
    out_shape = (
        jax.ShapeDtypeStruct((tm, d_attn), F32), jax.ShapeDtypeStruct((tm, d_attn), F32),
        jax.ShapeDtypeStruct((tm, d_attn), F32), jax.ShapeDtypeStruct((tm, d_attn), F32),
        jax.ShapeDtypeStruct((tm, d_model), F32), jax.ShapeDtypeStruct((tm, d_model), F32),
        jax.ShapeDtypeStruct(pst_tm.shape, F32), jax.ShapeDtypeStruct(cst_tm.shape, F32),
    )
    return pl.pallas_call(
        kern, out_shape=out_shape,
        scratch_shapes=[pltpu.VMEM((POOL_HALO * rs + tm, d_pool), F32), pltpu.VMEM((CONV_HALO * rs + tm, d_conv), F32)],
        compiler_params=pltpu.CompilerParams(vmem_limit_bytes=VMEM_LIMIT),
        name="proj_sample",
    )(x_tm, pst_tm, cst_tm, win, wpool, spool, wdw, bdw, clng, clnb, wpw2, pa, pc)


def _attn_sample_kernel(pt_ref, rel_ref, *refs, nb, bps, tdec, n_buckets):
    npg = 2 * bps
    k_refs, v_refs = refs[:npg], refs[npg:2 * npg]
    (q_ref, knew_ref, vnew_ref, bks_ref, bkn_ref, o_ref,
     blast_ref, cfar_ref, bown_ref, ms_ref, ls_ref, gs_ref, r_ref) = refs[2 * npg:]
    jj = pl.program_id(1)
    scale = HEAD_DIM ** -0.5
    bnn = (((2,), (1,)), ((0,), (0,)))
    bnt = (((2,), (2,)), ((0,), (0,)))

    @pl.when(jnp.logical_and(pl.program_id(0) == 0, jj == 0))
    def _():
        for h in range(N_HEADS):
            blast_ref[h] = jnp.zeros(blast_ref.shape[1:], F32)
            bown_ref[h] = jnp.full(bown_ref.shape[1:], NEG, F32)
            cfar_ref[h] = jnp.full(cfar_ref.shape[1:], rel_ref[n_buckets - 1, h], F32)

            def fill(bkt, c):
                val = rel_ref[bkt, h]
                blast_ref[h] = jnp.where(bks_ref[...] == bkt, val, blast_ref[h])
                bown_ref[h] = jnp.where(bkn_ref[...] == bkt, val, bown_ref[h])
                return c

            lax.fori_loop(0, n_buckets, fill, 0)

    q3 = q_ref[0]
    qs = (q3 * scale).astype(BF16)

    def block_tiles(refs, c):
        return jnp.concatenate([refs[2 * c][0, 0].astype(BF16), refs[2 * c + 1][0, 0].astype(BF16)], axis=-1)

    raw = [lax.dot_general(qs, block_tiles(k_refs, c), bnn, preferred_element_type=F32)
           for c in range(bps)]
    probs = []
    for c in range(bps):
        j = jj * bps + c
        gs_ref[j] = jnp.sum(raw[c], axis=-1, keepdims=True) * (1.0 / (MOBA_BLOCK * scale))
        if c == bps - 1:
            s = raw[c] + jnp.where(j == nb - 1, blast_ref[...], cfar_ref[...])
        else:
            s = raw[c] + cfar_ref[...]
        m = jnp.max(s, axis=-1, keepdims=True)
        p = jnp.exp(s - m)
        ms_ref[j] = m
        ls_ref[j] = jnp.sum(p, axis=-1, keepdims=True)
        probs.append(p.astype(BF16))
    for c in range(bps):
        r_ref[jj * bps + c] = lax.dot_general(probs[c], block_tiles(v_refs, c), bnt,
                                              preferred_element_type=F32)

    @pl.when(jj == nb // bps - 1)
    def _():
        knew, vnew = knew_ref[0], vnew_ref[0]
        qf = q3 * scale
        s_o = jnp.concatenate([jnp.sum(qf * knew[:, t:t + 1, :], axis=-1, keepdims=True) for t in range(tdec)],
                              axis=-1) + bown_ref[...]
        m_o = jnp.max(s_o, axis=-1, keepdims=True)
        p_o = jnp.exp(s_o - m_o)
        l_o = jnp.sum(p_o, axis=-1, keepdims=True)
        r_o = p_o[..., 0:1] * vnew[:, 0:1, :]
        for t in range(1, tdec):
            r_o = r_o + p_o[..., t:t + 1] * vnew[:, t:t + 1, :]
        g = gs_ref[...]
        n_iota = lax.broadcasted_iota(jnp.int32, g.shape, 0)
        sel = _top_k_mask(g, n_iota, min(MOBA_TOPK, nb), nb)
        ms = jnp.where(sel, ms_ref[...], NEG)
        mtot = jnp.maximum(jnp.max(ms, axis=0), m_o)
        w = jnp.where(sel, jnp.exp(ms - mtot[None]), 0.0)
        w_o = jnp.exp(m_o - mtot)
        ltot = w_o * l_o + jnp.sum(w * ls_ref[...], axis=0)
        acc = w_o * r_o + jnp.sum(w * r_ref[...], axis=0)
        o_ref[0] = acc / ltot


def _attn_sample(page_table, rel_bias, cache_kt, cache_vt, layer, q3, knew, vnew, bks, bkn):
    dec_b, _, qp, _ = q3.shape
    tdec = knew.shape[2]
    page = cache_kt.shape[4]
    assert 2 * page == MOBA_BLOCK, "one MoBA block is swept as two cache pages"
    nb = page_table.shape[1] // 2
    bps = math.gcd(SAMPLE_BLOCKS_PER_STEP, nb)
    kern = functools.partial(_attn_sample_kernel, nb=nb, bps=bps, tdec=tdec, n_buckets=rel_bias.shape[0])
    pg = lambda o: pl.BlockSpec((1, 1, N_HEADS, HEAD_DIM, page),
                                lambda b, jj, pt: (pt[b, 2 * bps * jj + o], layer, 0, 0, 0))
    full = lambda a: pl.BlockSpec(a.shape, lambda b, jj, pt: (0,) * a.ndim)
    per_b = lambda a: pl.BlockSpec((1,) + a.shape[1:], lambda b, jj, pt: (b,) + (0,) * (a.ndim - 1))
    stat = pltpu.VMEM((nb, N_HEADS, qp, 1), F32)
    grid_spec = pltpu.PrefetchScalarGridSpec(
        num_scalar_prefetch=1, grid=(dec_b, nb // bps),
        in_specs=([pl.BlockSpec(memory_space=pltpu.SMEM)] + [pg(o) for o in range(2 * bps)] * 2
                  + [per_b(q3), per_b(knew), per_b(vnew), full(bks), full(bkn)]),
        out_specs=pl.BlockSpec((1, N_HEADS, qp, HEAD_DIM), lambda b, jj, pt: (b, 0, 0, 0)),
        scratch_shapes=[pltpu.VMEM((N_HEADS,) + bks.shape, F32), pltpu.VMEM((N_HEADS,) + bks.shape, F32),
                        pltpu.VMEM((N_HEADS,) + bkn.shape, F32), stat, stat, stat,
                        pltpu.VMEM((nb, N_HEADS, qp, HEAD_DIM), F32)],
    )
    return pl.pallas_call(
        kern, grid_spec=grid_spec, out_shape=jax.ShapeDtypeStruct((dec_b, N_HEADS, qp, HEAD_DIM), F32),
        compiler_params=pltpu.CompilerParams(dimension_semantics=("arbitrary", "arbitrary"),
                                             vmem_limit_bytes=VMEM_LIMIT),
        name="attn_sample",
    )(page_table, rel_bias, *([cache_kt] * (2 * bps)), *([cache_vt] * (2 * bps)), q3, knew, vnew, bks, bkn)


def _merge_sample_kernel(yb_ref, gbs_ref, hac_ref, smb_ref, x_ref, pb_ref, wout_ref, lng_ref, lnb_ref, o_ref, *, alpha):
    o_ref[...] = _merge(yb_ref[...], gbs_ref[...], hac_ref[...], smb_ref[...], x_ref[...], pb_ref, wout_ref,
                        lng_ref, lnb_ref, alpha)


def _merge_sample(yb, gbs, hac, smb, x, pb, wout, lng, lnb, alpha):
    return pl.pallas_call(
        functools.partial(_merge_sample_kernel, alpha=alpha),
        out_shape=jax.ShapeDtypeStruct(x.shape, F32),
        compiler_params=pltpu.CompilerParams(vmem_limit_bytes=VMEM_LIMIT),
        name="merge_sample",
    )(yb, gbs, hac, smb, x, pb, wout, lng, lnb)


def _block_diag(w):
    g, c, d = w.shape
    out = jnp.zeros((g * c, g * d), w.dtype)
    for i in range(g):
        out = out.at[i * c:(i + 1) * c, i * d:(i + 1) * d].set(w[i])
    return out


def kernel(x_prompt, x_sample, cache_k, cache_v, page_table, state_pool, state_conv, rel_bias, w_in, w_pool, s_pool,
           w_dw, b_dw, conv_ln_g, conv_ln_b, w_pw2, p_a, p_b, p_c, w_out, ln_g, ln_b):
    depth = w_in.shape[0]
    bsz, t, d_model = x_prompt.shape
    dec_b, tdec, _ = x_sample.shape
    d_pool = w_pool.shape[1] * w_pool.shape[2]
    d_attn, d_conv = p_b.shape[1], p_c.shape[1]
    assert d_attn == N_HEADS * HEAD_DIM and t % MOBA_BLOCK == 0
    page = cache_k.shape[2]
    past_len = page_table.shape[1] * page
    assert past_len % MOBA_BLOCK == 0 and past_len >= POOL_HALO
    sp = _splits(d_pool, d_attn, d_conv, d_model)
    alpha = (2 * depth) ** 0.25
    n_buckets = rel_bias.shape[0]

    bucket = _bucket_table(n_buckets, 2 * MOBA_BLOCK + tdec)
    far_rel = int(np.argmax(bucket == n_buckets - 1))
    assert (bucket[far_rel:] == n_buckets - 1).all() and far_rel <= MOBA_BLOCK
    d_own = np.arange(MOBA_BLOCK)[None, :] - np.arange(MOBA_BLOCK)[:, None]
    bko = jnp.asarray(np.where(d_own >= 0, bucket[np.maximum(d_own, 0)], -1).astype(np.int32))
    bkp = jnp.asarray(bucket[d_own + MOBA_BLOCK].astype(np.int32))
    qp = -(-tdec // 8) * 8
    dq = np.minimum(np.arange(qp), tdec - 1)[:, None]
    bks = jnp.asarray(bucket[MOBA_BLOCK + dq - np.arange(MOBA_BLOCK)[None, :]].astype(np.int32))
    d_new = dq - np.arange(tdec)[None, :]
    bkn = jnp.asarray(np.where(d_new >= 0, bucket[np.maximum(d_new, 0)], -1).astype(np.int32))
    rel = rel_bias.astype(F32)
    cache_kt = cache_k.transpose(0, 1, 3, 4, 2)
    cache_vt = cache_v.transpose(0, 1, 3, 4, 2)
    kall = jnp.zeros((bsz, depth, d_attn, t), F32)
    vall = jnp.zeros((bsz, depth, d_attn, t), F32)

    xp = x_prompt
    xs = x_sample.transpose(1, 0, 2).reshape(tdec * dec_b, d_model)
    outs = {n: [] for n in ("ks", "vs", "pp", "ps", "cp", "cs")}
    for l in range(depth):
        win = w_in[l].astype(BF16)
        wpool = _block_diag(w_pool[l]).astype(BF16)
        wl = (win, wpool, s_pool[l][None], w_dw[l], b_dw[l][None], conv_ln_g[l][None], conv_ln_b[l][None],
              w_pw2[l].astype(BF16), p_a[l].astype(BF16), p_c[l].astype(BF16))
        pb, wout, lng, lnb = p_b[l].astype(BF16), w_out[l].astype(BF16), ln_g[l][None], ln_b[l][None]

        kall, vall, kb, vt, qt, kmean, gbs, smb, hac, pst, cst = _proj_prompt(xp, *wl, kall, vall, l, sp)
        xp = _attn_prompt(rel, qt, kb, vt, kmean.reshape(bsz, -1, d_attn), bko, bkp, xp, hac, smb, gbs, pb, wout,
                          lng, lnb, alpha)
        outs["pp"].append(pst)
        outs["cp"].append(cst)

        pst_tm = state_pool[:, l].transpose(1, 0, 2).reshape(-1, d_pool)
        cst_tm = state_conv[:, l].transpose(1, 0, 2).reshape(-1, d_conv)
        q_s, k_s, v_s, gbs_s, smb_s, hac_s, pst_s, cst_s = _proj_sample(xs, pst_tm, cst_tm, *wl, sp, dec_b)
        to_h = lambda a: a.reshape(tdec, dec_b, N_HEADS, HEAD_DIM).transpose(1, 2, 0, 3)
        q_h, k_h, v_h = to_h(q_s), to_h(k_s), to_h(v_s)
        q3 = jnp.pad(q_h, ((0, 0), (0, 0), (0, qp - tdec), (0, 0)))
        yb_s = _attn_sample(page_table, rel, cache_kt, cache_vt, l, q3, k_h, v_h, bks, bkn)
        yb_tm = yb_s[:, :, :tdec].transpose(2, 0, 1, 3).reshape(tdec * dec_b, d_attn)
        k_b, v_b = k_h.transpose(0, 2, 1, 3), v_h.transpose(0, 2, 1, 3)
        xs = _merge_sample(yb_tm, gbs_s, hac_s, smb_s, xs, pb, wout, lng, lnb, alpha)
        outs["ks"].append(k_b)
        outs["vs"].append(v_b)
        outs["ps"].append(pst_s.reshape(-1, dec_b, d_pool).transpose(1, 0, 2))
        outs["cs"].append(cst_s.reshape(-1, dec_b, d_conv).transpose(1, 0, 2))

    st = lambda n: jnp.stack(outs[n], axis=1)
    y_sample = xs.reshape(tdec, dec_b, d_model).transpose(1, 0, 2)
    to_out = lambda a: a.reshape(bsz, depth, N_HEADS, HEAD_DIM, t).transpose(0, 1, 4, 2, 3)
    return (xp, y_sample, to_out(kall), to_out(vall), st("ks"), st("vs"), st("pp"), st("ps"), st("cp"), st("cs"))
```
